```python
import jax
import jax.numpy as jnp
from jax import lax
import numpy as np

D_MODEL = 1024
BATCH = 32
SEQ = 2048
DEPTH = 2
DEC_BATCH = 128
DEC_SEQ = 4
PAST_LEN = 16384
PAGE_SIZE = 128

N_EVEN = (DEPTH + 1) // 2
N_ODD = DEPTH // 2

SSM_HEAD_DIM = 64
SSM_HEADS = 16
SSM_INNER = SSM_HEADS * SSM_HEAD_DIM
SSM_GROUPS = 2
SSM_STATE = 128
SSM_CONV = 4
SSM_CONV_DIM = SSM_INNER + 2 * SSM_GROUPS * SSM_STATE
SSM_CHUNK = 128

SB_HEADS = 8
SB_KV_HEADS = 4
SB_GROUP = SB_HEADS // SB_KV_HEADS
SB_HEAD_DIM = 64
SB_BLOCK = 128

HG_HEADS = 4
HG_DK = 128
HG_DV = 128
HG_CHUNK = 32

MLA_HEADS = 8
MLA_Q_LORA = 384
MLA_KV_LORA = 256
MLA_NOPE = 64
MLA_ROPE = 32
MLA_V = 64
MLA_BLOCK = 128
MLA_SCALE = (MLA_NOPE + MLA_ROPE) ** -0.5
ROPE_THETA = 10000.0

MEM_LEN = 256
MEM_HEADS = 4
MEM_HEAD_DIM = 128
MEM_INNER = MEM_HEADS * MEM_HEAD_DIM

FFN_HIDDEN = ((8 * D_MODEL // 3 + 255) // 256) * 256

DEEPNORM_ALPHA = (2 * DEPTH) ** 0.25
DEEPNORM_BETA = (8 * DEPTH) ** -0.25

EVEN_SPLITS = (SSM_INNER, SSM_CONV_DIM, SSM_HEADS, SB_HEADS * SB_HEAD_DIM, SB_KV_HEADS * SB_HEAD_DIM, SB_KV_HEADS * SB_HEAD_DIM)
EVEN_IN = sum(EVEN_SPLITS)
EVEN_OUT = SSM_INNER + SB_HEADS * SB_HEAD_DIM
ODD_SPLITS = (HG_HEADS * HG_DK, HG_HEADS * HG_DK, HG_HEADS * HG_DV, HG_HEADS * HG_DV, MLA_Q_LORA, MLA_KV_LORA, MLA_ROPE)
ODD_IN = sum(ODD_SPLITS)
ODD_OUT = HG_HEADS * HG_DV + MLA_HEADS * MLA_V

kernel_name = 'hybrid_ssd_stickbreak_hgrn2_mla_decode_step'

F32 = jnp.float32


def _split(x, sizes):
    return jnp.split(x, np.cumsum(sizes)[:-1].tolist(), axis=-1)


def _layer_norm(x, g, b, eps=1e-5):
    xf = x.astype(F32)
    xc = xf - jnp.mean(xf, -1, keepdims=True)
    var = jnp.mean(xc * xc, -1, keepdims=True)
    return (xc * lax.rsqrt(var + eps) * g.astype(F32) + b.astype(F32)).astype(x.dtype)


def _rms_norm(x, w, eps=1e-6):
    xf = x.astype(F32)
    return xf * lax.rsqrt(jnp.mean(xf * xf, -1, keepdims=True) + eps) * w.astype(F32)


def _rope(x, pos):
    half = x.shape[-1] // 2
    inv = ROPE_THETA ** (-jnp.arange(half, dtype=F32) / half)
    ang = pos[:, None] * inv[None, :]
    cos, sin = jnp.cos(ang)[:, None, :], jnp.sin(ang)[:, None, :]
    x1, x2 = x[..., :half], x[..., half:]
    return jnp.concatenate([x1 * cos - x2 * sin, x2 * cos + x1 * sin], axis=-1)


def _gather_pages(pool, layer, page_table):
    g = pool[layer, page_table]
    return g.reshape((g.shape[0], g.shape[1] * g.shape[2]) + g.shape[3:])


def _causal_dwconv(u, buf, w, b):
    up = jnp.concatenate([buf.astype(u.dtype), u], axis=1)
    out = lax.conv_general_dilated(up, w[:, None, :].astype(u.dtype), window_strides=(1,), padding='VALID',
                                   dimension_numbers=('NWC', 'WIO', 'NWC'), feature_group_count=u.shape[-1])
    return out + b, up[:, -(SSM_CONV - 1):]


def _ssd_scan(x, dt, a_neg, bmat, cmat, s0):
    bsz, L, nh, p = x.shape
    g, n = bmat.shape[2], bmat.shape[3]
    r = nh // g
    q = min(SSM_CHUNK, L)
    nc = L // q
    xc = x.reshape(bsz, nc, q, g, r, p)
    dtc = dt.reshape(bsz, nc, q, g, r)
    bc = bmat.reshape(bsz, nc, q, g, n)
    cc = cmat.reshape(bsz, nc, q, g, n)
    acum = jnp.cumsum(dtc * a_neg.reshape(g, r), axis=2)
    tri = jnp.tril(jnp.ones((q, q), bool))[:, :, None, None]
    diff = acum[:, :, :, None] - acum[:, :, None, :]
    decay = jnp.exp(jnp.where(tri, diff, -jnp.inf))
    cb = jnp.einsum('bctgn,bcsgn->bctsg', cc, bc)
    wts = cb[..., None] * decay * dtc[:, :, None]
    y = jnp.einsum('bctsgr,bcsgrp->bctgrp', wts, xc)
    decay_end = jnp.exp(acum[:, :, -1:] - acum)
    chunk_states = jnp.einsum('bcsgn,bcsgr,bcsgrp->bcgrpn', bc, decay_end * dtc, xc)
    chunk_decay = jnp.exp(acum[:, :, -1])

    def step(s, inp):
        st, dec = inp
        return s * dec[..., None, None] + st, s

    s_fin, s_prev = lax.scan(step, s0.reshape(bsz, g, r, p, n),
                             (jnp.moveaxis(chunk_states, 1, 0), jnp.moveaxis(chunk_decay, 1, 0)))
    y = y + jnp.einsum('bctgn,bcgrpn,bctgr->bctgrp', cc, jnp.moveaxis(s_prev, 0, 1), jnp.exp(acum))
    return y.reshape(bsz, L, nh, p), s_fin.reshape(bsz, nh, p, n)


def _sb_block(qb, k, v, q_start):
    logits = jnp.einsum('bqkgd,bskd->bkgqs', qb, k)
    q_pos = q_start + jnp.arange(qb.shape[1])
    valid = jnp.arange(k.shape[1])[None, :] < q_pos[:, None]
    log_beta = jax.nn.log_sigmoid(logits)
    log_keep = jnp.where(valid, log_beta - logits, 0.0)
    log_w = log_beta + lax.cumsum(log_keep, axis=4, reverse=True) - log_keep
    w = jnp.where(valid, jnp.exp(log_w), 0.0)
    return jnp.einsum('bkgqs,bskd->bqkgd', w, v)


def _stick_breaking(q, k, v, q_offset):
    bsz, L = q.shape[:2]
    blk = min(SB_BLOCK, L)
    nb = L // blk
    qb = (q.astype(F32) * SB_HEAD_DIM ** -0.5).reshape(bsz, nb, blk, SB_KV_HEADS, SB_GROUP, SB_HEAD_DIM)
    starts = q_offset + blk * jnp.arange(nb)
    kf, vf = k.astype(F32), v.astype(F32)
    out = lax.map(lambda a: _sb_block(a[0], kf, vf, a[1]), (jnp.moveaxis(qb, 1, 0), starts))
    return jnp.moveaxis(out, 0, 1).reshape(bsz, L, SB_HEADS * SB_HEAD_DIM)


def _hgrn2_scan(q, f_pre, v, lb, s0):
    bsz, L = q.shape[:2]
    t = min(HG_CHUNK, L)
    nc = L // t
    logf = jnp.log(lb + (1.0 - lb) * jax.nn.sigmoid(f_pre))
    k = (1.0 - lb) * jax.nn.sigmoid(-f_pre)
    rs = lambda a: a.reshape(bsz, nc, t, HG_HEADS, a.shape[-1])
    qc, kc, vc = rs(q), rs(k), rs(v)
    bcum = jnp.cumsum(rs(logf), axis=2)
    q_dec = qc * jnp.exp(bcum)
    k_inv = kc * jnp.exp(-bcum)
    tri = jnp.tril(jnp.ones((t, t), bool))
    att = jnp.where(tri, jnp.einsum('bcthk,bcshk->bchts', q_dec, k_inv), 0.0)
    o = jnp.einsum('bchts,bcshv->bcthv', att, vc)
    k_end = kc * jnp.exp(bcum[:, :, -1:] - bcum)
    chunk_states = jnp.einsum('bcshk,bcshv->bchkv', k_end, vc)
    chunk_decay = jnp.exp(bcum[:, :, -1])

    def step(s, inp):
        st, dec = inp
        return s * dec[..., None] + st, s

    s_fin, s_prev = lax.scan(step, s0, (jnp.moveaxis(chunk_states, 1, 0), jnp.moveaxis(chunk_decay, 1, 0)))
    o = o + jnp.einsum('bcthk,bchkv->bcthv', q_dec, jnp.moveaxis(s_prev, 0, 1))
    return o.reshape(bsz, L, HG_HEADS, HG_DV), s_fin


def _mla_block(ql, qr, c, r, q_start):
    logits = (jnp.einsum('bthc,bsc->bhts', ql, c) + jnp.einsum('bthr,bsr->bhts', qr, r)) * MLA_SCALE
    q_pos = q_start + jnp.arange(ql.shape[1])
    valid = jnp.arange(c.shape[1])[None, :] <= q_pos[:, None]
    p = jax.nn.softmax(jnp.where(valid, logits, -jnp.inf), axis=-1)
    return jnp.einsum('bhts,bsc->bthc', p, c)


def _mla_attend(ql, qr, c, r, q_offset):
    bsz, L = ql.shape[:2]
    blk = min(MLA_BLOCK, L)
    nb = L // blk
    qlb = jnp.moveaxis(ql.reshape(bsz, nb, blk, MLA_HEADS, MLA_KV_LORA), 1, 0)
    qrb = jnp.moveaxis(qr.reshape(bsz, nb, blk, MLA_HEADS, MLA_ROPE), 1, 0)
    starts = q_offset + blk * jnp.arange(nb)
    out = lax.map(lambda a: _mla_block(a[0], a[1], c, r, a[2]), (qlb, qrb, starts))
    return jnp.moveaxis(out, 0, 1).reshape(bsz, L, MLA_HEADS, MLA_KV_LORA)


def _even_mixer(h, pos_start, w_in, w_out, conv_w, conv_b, dt_bias, a_log, d_skip, norm_w, conv_buf, s0, k_past, v_past):
    bsz, L, _ = h.shape
    z, xbc, dt_raw, q, k, v = _split(h @ w_in, EVEN_SPLITS)
    xbc_c, new_buf = _causal_dwconv(xbc, conv_buf, conv_w, conv_b)
    xs, bm, cm = _split(jax.nn.silu(xbc_c.astype(F32)), (SSM_INNER, SSM_GROUPS * SSM_STATE, SSM_GROUPS * SSM_STATE))
    xs = xs.reshape(bsz, L, SSM_HEADS, SSM_HEAD_DIM)
    bm = bm.reshape(bsz, L, SSM_GROUPS, SSM_STATE)
    cm = cm.reshape(bsz, L, SSM_GROUPS, SSM_STATE)
    dt = jax.nn.softplus(dt_raw.astype(F32) + dt_bias.astype(F32))
    y_a, s_new = _ssd_scan(xs, dt, -jnp.exp(a_log.astype(F32)), bm, cm, s0.astype(F32))
    y_a = (y_a + d_skip.astype(F32)[:, None] * xs).reshape(bsz, L, SSM_INNER)
    y_a = _rms_norm(y_a * jax.nn.silu(z.astype(F32)), norm_w)
    q = q.reshape(bsz, L, SB_HEADS, SB_HEAD_DIM)
    k = k.reshape(bsz, L, SB_KV_HEADS, SB_HEAD_DIM)
    v = v.reshape(bsz, L, SB_KV_HEADS, SB_HEAD_DIM)
    k_all = k if k_past is None else jnp.concatenate([k_past.astype(k.dtype), k], axis=1)
    v_all = v if v_past is None else jnp.concatenate([v_past.astype(v.dtype), v], axis=1)
    y_b = _stick_breaking(q, k_all, v_all, pos_start)
    y = jnp.concatenate([y_a, y_b], axis=-1).astype(h.dtype) @ w_out
    return y, new_buf, s_new, k, v


def _odd_mixer(h, pos_start, lb, w_in, w_out, hg_norm_w, q_norm_w, kv_norm_w, w_uq, w_uk, w_uv, s0, c_past, r_past):
    bsz, L, _ = h.shape
    hq, hf, hi, hg, cq, ckv, kr = _split(h @ w_in, ODD_SPLITS)
    shp = lambda a, d: a.astype(F32).reshape(bsz, L, HG_HEADS, d)
    o_c, s_new = _hgrn2_scan(shp(hq, HG_DK), shp(hf, HG_DK), shp(hi, HG_DV), lb.reshape(HG_HEADS, HG_DK), s0.astype(F32))
    o_c = (_rms_norm(o_c, hg_norm_w) * jax.nn.silu(shp(hg, HG_DV))).reshape(bsz, L, HG_HEADS * HG_DV)
    pos = jnp.arange(L, dtype=F32) + pos_start
    qf = jnp.einsum('blq,qhe->blhe', _rms_norm(cq, q_norm_w), w_uq.astype(F32))
    q_nope, q_rope = qf[..., :MLA_NOPE], _rope(qf[..., MLA_NOPE:], pos)
    c_new = _rms_norm(ckv, kv_norm_w)
    r_new = _rope(kr.astype(F32)[:, :, None, :], pos)[:, :, 0]
    q_lat = jnp.einsum('blhn,chn->blhc', q_nope, w_uk.astype(F32))
    c_all = c_new if c_past is None else jnp.concatenate([c_past.astype(F32), c_new], axis=1)
    r_all = r_new if r_past is None else jnp.concatenate([r_past.astype(F32), r_new], axis=1)
    ctx = _mla_attend(q_lat, q_rope, c_all, r_all, pos_start)
    o_d = jnp.einsum('blhc,chv->blhv', ctx, w_uv.astype(F32)).reshape(bsz, L, MLA_HEADS * MLA_V)
    y = jnp.concatenate([o_c, o_d], axis=-1).astype(h.dtype) @ w_out
    return y, s_new, c_new, r_new


def _hgrn_lower_bound(gamma, layer):
    p = jax.nn.softmax(gamma.astype(F32), axis=0)
    return jnp.cumsum(p, axis=0)[layer] - p[0]


def _mem_kv(mem, wk, wv):
    b, m, _ = mem.shape
    return ((mem @ wk).reshape(b, m, MEM_HEADS, MEM_HEAD_DIM), (mem @ wv).reshape(b, m, MEM_HEADS, MEM_HEAD_DIM))


def _cross_attend(x, mk, mv, wq, wo):
    b, L, _ = x.shape
    q = (x @ wq).astype(F32).reshape(b, L, MEM_HEADS, MEM_HEAD_DIM) * MEM_HEAD_DIM ** -0.5
    p = jax.nn.softmax(jnp.einsum('blhd,bmhd->bhlm', q, mk.astype(F32)), axis=-1)
    o = jnp.einsum('bhlm,bmhd->blhd', p, mv.astype(F32)).reshape(b, L, MEM_INNER)
    return o.astype(x.dtype) @ wo


def _swiglu(x, wg, wu, wd):
    return (jax.nn.silu(x @ wg) * (x @ wu)) @ wd


def setup_inputs(seed: int = 0) -> dict:
    key = jax.random.key(seed)
    ks = iter(jax.random.split(key, 64))
    n_pages = PAST_LEN // PAGE_SIZE
    pool = (DEC_BATCH * n_pages * 5) // 4

    def nrm(shape, scale=1.0):
        return jax.random.normal(next(ks), shape, F32) * scale

    inp = {}
    inp['x_prompt'] = nrm((BATCH, SEQ, D_MODEL))
    inp['x_sample'] = nrm((DEC_BATCH, DEC_SEQ, D_MODEL))
    inp['mem_prompt'] = nrm((BATCH, MEM_LEN, D_MODEL))
    inp['state_ssm'] = nrm((N_EVEN, DEC_BATCH, SSM_HEADS, SSM_HEAD_DIM, SSM_STATE), 0.3)
    inp['state_conv'] = nrm((N_EVEN, DEC_BATCH, SSM_CONV - 1, SSM_CONV_DIM))
    inp['cache_sb_k'] = nrm((N_EVEN, pool, PAGE_SIZE, SB_KV_HEADS, SB_HEAD_DIM))
    inp['cache_sb_v'] = nrm((N_EVEN, pool, PAGE_SIZE, SB_KV_HEADS, SB_HEAD_DIM))
    inp['state_hgrn'] = nrm((N_ODD, DEC_BATCH, HG_HEADS, HG_DK, HG_DV), 0.5)
    inp['cache_mla_latent'] = nrm((N_ODD, pool, PAGE_SIZE, MLA_KV_LORA))
    inp['cache_mla_rope'] = nrm((N_ODD, pool, PAGE_SIZE, MLA_ROPE))
    inp['cache_mem_k'] = nrm((DEPTH, DEC_BATCH, MEM_LEN, MEM_HEADS, MEM_HEAD_DIM))
    inp['cache_mem_v'] = nrm((DEPTH, DEC_BATCH, MEM_LEN, MEM_HEADS, MEM_HEAD_DIM), DEEPNORM_BETA)
    perm = jax.random.permutation(next(ks), pool)
    inp['page_table'] = perm[: DEC_BATCH * n_pages].reshape(DEC_BATCH, n_pages).astype(jnp.int32)
    inp['w_in_even'] = nrm((N_EVEN, D_MODEL, EVEN_IN), D_MODEL ** -0.5)
    inp['w_out_even'] = nrm((N_EVEN, EVEN_OUT, D_MODEL), EVEN_OUT ** -0.5 * DEEPNORM_BETA)
    inp['ssm_conv_w'] = nrm((N_EVEN, SSM_CONV, SSM_CONV_DIM), SSM_CONV ** -0.5)
    inp['ssm_conv_b'] = nrm((N_EVEN, SSM_CONV_DIM), 0.02)
    dt0 = jnp.exp(jax.random.uniform(next(ks), (N_EVEN, SSM_HEADS), F32, float(np.log(1e-3)), float(np.log(1e-1))))
    inp['ssm_dt_bias'] = dt0 + jnp.log(-jnp.expm1(-dt0))
    inp['ssm_a_log'] = jnp.log(jax.random.uniform(next(ks), (N_EVEN, SSM_HEADS), F32, 1.0, 16.0))
    inp['ssm_d'] = 1.0 + nrm((N_EVEN, SSM_HEADS), 0.02)
    inp['ssm_norm_w'] = 1.0 + nrm((N_EVEN, SSM_INNER), 0.02)
    inp['w_in_odd'] = nrm((N_ODD, D_MODEL, ODD_IN), D_MODEL ** -0.5)
    inp['w_out_odd'] = nrm((N_ODD, ODD_OUT, D_MODEL), ODD_OUT ** -0.5 * DEEPNORM_BETA)
    inp['hg_lower_bound'] = nrm((DEPTH, HG_HEADS * HG_DK), 0.1)
    inp['hg_norm_w'] = 1.0 + nrm((N_ODD, HG_HEADS, HG_DV), 0.02)
    inp['mla_q_norm_w'] = 1.0 + nrm((N_ODD, MLA_Q_LORA), 0.02)
    inp['mla_kv_norm_w'] = 1.0 + nrm((N_ODD, MLA_KV_LORA), 0.02)
    inp['mla_w_uq'] = nrm((N_ODD, MLA_Q_LORA, MLA_HEADS, MLA_NOPE + MLA_ROPE), MLA_Q_LORA ** -0.5)
    inp['mla_w_uk'] = nrm((N_ODD, MLA_KV_LORA, MLA_HEADS, MLA_NOPE), MLA_KV_LORA ** -0.5)
    inp['mla_w_uv'] = nrm((N_ODD, MLA_KV_LORA, MLA_HEADS, MLA_V), MLA_KV_LORA ** -0.5)
    inp['w_mem_q'] = nrm((DEPTH, D_MODEL, MEM_INNER), D_MODEL ** -0.5)
    inp['w_mem_k'] = nrm((DEPTH, D_MODEL, MEM_INNER), D_MODEL ** -0.5)
    inp['w_mem_v'] = nrm((DEPTH, D_MODEL, MEM_INNER), D_MODEL ** -0.5 * DEEPNORM_BETA)
    inp['w_mem_o'] = nrm((DEPTH, MEM_INNER, D_MODEL), MEM_INNER ** -0.5 * DEEPNORM_BETA)
    inp['ffn_w_gate'] = nrm((DEPTH, D_MODEL, FFN_HIDDEN), D_MODEL ** -0.5)
    inp['ffn_w_up'] = nrm((DEPTH, D_MODEL, FFN_HIDDEN), D_MODEL ** -0.5)
    inp['ffn_w_down'] = nrm((DEPTH, FFN_HIDDEN, D_MODEL), FFN_HIDDEN ** -0.5 * DEEPNORM_BETA)
    inp['ln_g'] = 1.0 + nrm((DEPTH, 3, D_MODEL), 0.02)
    inp['ln_b'] = nrm((DEPTH, 3, D_MODEL), 0.02)
    return inp


def reference(x_prompt, x_sample, mem_prompt, state_ssm, state_conv, cache_sb_k, cache_sb_v, state_hgrn,
              cache_mla_latent, cache_mla_rope, cache_mem_k, cache_mem_v, page_table,
              w_in_even, w_out_even, ssm_conv_w, ssm_conv_b, ssm_dt_bias, ssm_a_log, ssm_d, ssm_norm_w,
              w_in_odd, w_out_odd, hg_lower_bound, hg_norm_w, mla_q_norm_w, mla_kv_norm_w, mla_w_uq, mla_w_uk, mla_w_uv,
              w_mem_q, w_mem_k, w_mem_v, w_mem_o, ffn_w_gate, ffn_w_up, ffn_w_down, ln_g, ln_b):
    past_len = page_table.shape[1] * PAGE_SIZE

    def run(x, sample):
        out_dtype = x.dtype
        bsz = x.shape[0]
        pos_start = past_len if sample else 0
        ssm_l, conv_l, sbk_l, sbv_l, hg_l, lat_l, rope_l, mk_l, mv_l = ([] for _ in range(9))
        for l in range(DEPTH):
            i = l // 2
            if l % 2 == 0:
                if sample:
                    conv0, s0 = state_conv[i], state_ssm[i]
                    k_past = _gather_pages(cache_sb_k, i, page_table)
                    v_past = _gather_pages(cache_sb_v, i, page_table)
                else:
                    conv0 = jnp.zeros((bsz, SSM_CONV - 1, SSM_CONV_DIM), out_dtype)
                    s0 = jnp.zeros((bsz, SSM_HEADS, SSM_HEAD_DIM, SSM_STATE), F32)
                    k_past = v_past = None
                y, conv1, s1, k_new, v_new = _even_mixer(
                    x, pos_start, w_in_even[i], w_out_even[i], ssm_conv_w[i], ssm_conv_b[i], ssm_dt_bias[i],
                    ssm_a_log[i], ssm_d[i], ssm_norm_w[i], conv0, s0, k_past, v_past)
                ssm_l.append(s1)
                conv_l.append(conv1)
                sbk_l.append(k_new)
                sbv_l.append(v_new)
            else:
                if sample:
                    s0 = state_hgrn[i]
                    c_past = _gather_pages(cache_mla_latent, i, page_table)
                    r_past = _gather_pages(cache_mla_rope, i, page_table)
                else:
                    s0 = jnp.zeros((bsz, HG_HEADS, HG_DK, HG_DV), F32)
                    c_past = r_past = None
                y, s1, c_new, r_new = _odd_mixer(
                    x, pos_start, _hgrn_lower_bound(hg_lower_bound, l), w_in_odd[i], w_out_odd[i], hg_norm_w[i],
                    mla_q_norm_w[i], mla_kv_norm_w[i], mla_w_uq[i], mla_w_uk[i], mla_w_uv[i], s0, c_past, r_past)
                hg_l.append(s1)
                lat_l.append(c_new)
                rope_l.append(r_new)
            x = _layer_norm(DEEPNORM_ALPHA * x + y, ln_g[l, 0], ln_b[l, 0])
            if sample:
                mk, mv = cache_mem_k[l], cache_mem_v[l]
            else:
                mk, mv = _mem_kv(mem_prompt, w_mem_k[l], w_mem_v[l])
                mk_l.append(mk)
                mv_l.append(mv)
            x = _layer_norm(DEEPNORM_ALPHA * x + _cross_attend(x, mk, mv, w_mem_q[l], w_mem_o[l]), ln_g[l, 1], ln_b[l, 1])
            x = _layer_norm(DEEPNORM_ALPHA * x + _swiglu(x, ffn_w_gate[l], ffn_w_up[l], ffn_w_down[l]), ln_g[l, 2], ln_b[l, 2])
        stk = lambda a: jnp.stack([t.astype(out_dtype) for t in a], axis=0)
        return x, stk(ssm_l), stk(conv_l), stk(sbk_l), stk(sbv_l), stk(hg_l), stk(lat_l), stk(rope_l), mk_l, mv_l

    y_prompt, ssm_p, conv_p, sbk_p, sbv_p, hg_p, lat_p, rope_p, mk_p, mv_p = run(x_prompt, False)
    y_sample, ssm_s, conv_s, sbk_s, sbv_s, hg_s, lat_s, rope_s, _, _ = run(x_sample, True)
    memk_p = jnp.stack(mk_p, axis=0).astype(x_prompt.dtype)
    memv_p = jnp.stack(mv_p, axis=0).astype(x_prompt.dtype)
    return (y_prompt, y_sample, ssm_p, conv_p, sbk_p, sbv_p, hg_p, lat_p, rope_p, memk_p, memv_p,
            ssm_s, conv_s, sbk_s, sbv_s, hg_s, lat_s, rope_s)
```

```python
import functools

import numpy as np
import jax
import jax.numpy as jnp
from jax import lax
from jax.experimental import pallas as pl
from jax.experimental.pallas import tpu as pltpu

F32 = jnp.float32
BF16 = jnp.bfloat16

D_MODEL = 1024
DEPTH = 2
PAGE_SIZE = 128

SSM_HEAD_DIM = 64
SSM_HEADS = 16
SSM_INNER = SSM_HEADS * SSM_HEAD_DIM
SSM_GROUPS = 2
SSM_STATE = 128
SSM_CONV = 4
SSM_CONV_DIM = SSM_INNER + 2 * SSM_GROUPS * SSM_STATE
SSM_CHUNK = 128

SB_HEADS = 8
SB_KV_HEADS = 4
SB_HEAD_DIM = 64
SB_Q_DIM = SB_HEADS * SB_HEAD_DIM
SB_KV_DIM = SB_KV_HEADS * SB_HEAD_DIM

HG_HEADS = 4
HG_DK = 128
HG_DV = 128
HG_CHUNK = 32
HG_DIM = HG_HEADS * HG_DK

MLA_HEADS = 8
MLA_Q_LORA = 384
MLA_KV_LORA = 256
MLA_NOPE = 64
MLA_ROPE = 32
MLA_V = 64
MLA_SCALE = (MLA_NOPE + MLA_ROPE) ** -0.5
ROPE_THETA = 10000.0

MEM_HEADS = 4
MEM_HEAD_DIM = 128
MEM_INNER = MEM_HEADS * MEM_HEAD_DIM

DEEPNORM_ALPHA = (2 * DEPTH) ** 0.25

LANES = 128
VMEM_LIMIT_BYTES = 48 * 1024 * 1024
NEG_BIG = -1e30
SB_DEAD_LOG = -120.0


def _cparams(*sem):
    return pltpu.CompilerParams(dimension_semantics=sem, vmem_limit_bytes=VMEM_LIMIT_BYTES)


def _sigmoid(x):
    return 1.0 / (1.0 + jnp.exp(-x))


def _silu(x):
    return x * _sigmoid(x)


def _softplus(x):
    return jnp.maximum(x, 0.0) + jnp.log1p(jnp.exp(-jnp.abs(x)))


def _dot(a, b):
    return jnp.dot(a, b, preferred_element_type=F32)


def _dot_nt(a, b):
    return lax.dot_general(a, b, (((1,), (1,)), ((), ())), preferred_element_type=F32)


def _dot_tn(a, b):
    return lax.dot_general(a, b, (((0,), (0,)), ((), ())), preferred_element_type=F32)


def _split3(x):
    hi = x.astype(BF16)
    r1 = x - hi.astype(F32)
    mid = r1.astype(BF16)
    lo = (r1 - mid.astype(F32)).astype(BF16)
    return hi, mid, lo


def _ones_dot(mask01, x):
    hi, mid, lo = _split3(x)
    return _dot(mask01, hi) + _dot(mask01, mid) + _dot(mask01, lo)


def _dot_ones(x, mask01):
    hi, mid, lo = _split3(x)
    return _dot(hi, mask01) + _dot(mid, mask01) + _dot(lo, mask01)


def _layer_norm(x, g, b):
    xc = x - jnp.mean(x, axis=-1, keepdims=True)
    var = jnp.mean(xc * xc, axis=-1, keepdims=True)
    return xc * lax.rsqrt(var + 1e-5) * g + b


def _rms(x, w, eps=1e-6):
    return x * lax.rsqrt(jnp.mean(x * x, axis=-1, keepdims=True) + eps) * w


def _multi_linear_kernel(x_ref, *refs, n):
    x = x_ref[...].astype(BF16)
    for w_ref, o_ref in zip(refs[:n], refs[n:]):
        o_ref[...] = _dot(x, w_ref[...]).astype(o_ref.dtype)


def multi_linear(x, ws, out_dtypes, tm=512):
    m, k = x.shape
    tm = min(tm, m)
    n = len(ws)
    in_specs = [pl.BlockSpec((tm, k), lambda i: (i, 0))]
    in_specs += [pl.BlockSpec(w.shape, lambda i: (0, 0)) for w in ws]
    out_specs = [pl.BlockSpec((tm, w.shape[1]), lambda i: (i, 0)) for w in ws]
    out_shape = [jax.ShapeDtypeStruct((m, w.shape[1]), dt) for w, dt in zip(ws, out_dtypes)]
    return pl.pallas_call(
        functools.partial(_multi_linear_kernel, n=n),
        grid=(m // tm,), in_specs=in_specs, out_specs=out_specs, out_shape=out_shape,
        compiler_params=_cparams("parallel"), name="multi_linear")(x, *ws)


def _proj_ln_kernel(*refs, n):
    a_refs, w_refs = refs[:n], refs[n:2 * n]
    r_ref, g_ref, b_ref, o_ref = refs[2 * n:]
    y = _dot(a_refs[0][...].astype(BF16), w_refs[0][...])
    for a_ref, w_ref in zip(a_refs[1:], w_refs[1:]):
        y = y + _dot(a_ref[...].astype(BF16), w_ref[...])
    o_ref[...] = _layer_norm(DEEPNORM_ALPHA * r_ref[...] + y, g_ref[...], b_ref[...])


def proj_ln(a_list, w_list, resid, g, b, tm=512):
    m, d = resid.shape
    tm = min(tm, m)
    n = len(a_list)
    in_specs = [pl.BlockSpec((tm, a.shape[1]), lambda i: (i, 0)) for a in a_list]
    in_specs += [pl.BlockSpec(w.shape, lambda i: (0, 0)) for w in w_list]
    in_specs += [pl.BlockSpec((tm, d), lambda i: (i, 0)),
                 pl.BlockSpec((1, d), lambda i: (0, 0)), pl.BlockSpec((1, d), lambda i: (0, 0))]
    return pl.pallas_call(
        functools.partial(_proj_ln_kernel, n=n),
        grid=(m // tm,), in_specs=in_specs, out_specs=pl.BlockSpec((tm, d), lambda i: (i, 0)),
        out_shape=jax.ShapeDtypeStruct((m, d), F32),
        compiler_params=_cparams("parallel"), name="proj_ln")(*a_list, *w_list, resid, g, b)


def _ffn_kernel(x_ref, wg_ref, wu_ref, wd_ref, g_ref, b_ref, o_ref, acc_ref):
    j = pl.program_id(1)
    x = x_ref[...]
    xb = x.astype(BF16)
    gate = _dot(xb, wg_ref[...])
    up = _dot(xb, wu_ref[...])
    part = _dot((_silu(gate) * up).astype(BF16), wd_ref[...])

    @pl.when(j == 0)
    def _():
        acc_ref[...] = part

    @pl.when(j > 0)
    def _():
        acc_ref[...] += part

    @pl.when(j == pl.num_programs(1) - 1)
    def _():
        o_ref[...] = _layer_norm(DEEPNORM_ALPHA * x + acc_ref[...], g_ref[...], b_ref[...])


def ffn_ln(x, wg, wu, wd, g, b, tm=512, th=256):
    m, d = x.shape
    hdim = wg.shape[1]
    tm = min(tm, m)
    return pl.pallas_call(
        _ffn_kernel,
        grid=(m // tm, hdim // th),
        in_specs=[pl.BlockSpec((tm, d), lambda i, j: (i, 0)),
                  pl.BlockSpec((d, th), lambda i, j: (0, j)),
                  pl.BlockSpec((d, th), lambda i, j: (0, j)),
                  pl.BlockSpec((th, d), lambda i, j: (j, 0)),
                  pl.BlockSpec((1, d), lambda i, j: (0, 0)),
                  pl.BlockSpec((1, d), lambda i, j: (0, 0))],
        out_specs=pl.BlockSpec((tm, d), lambda i, j: (i, 0)),
        out_shape=jax.ShapeDtypeStruct((m, d), F32),
        scratch_shapes=[pltpu.VMEM((tm, d), F32)],
        compiler_params=_cparams("parallel", "arbitrary"), name="ffn_ln")(x, wg, wu, wd, g, b)


def _cross_kernel(q_ref, k_ref, v_ref, o_ref):
    scale = MEM_HEAD_DIM ** -0.5
    for h in range(MEM_HEADS):
        sl = slice(h * MEM_HEAD_DIM, (h + 1) * MEM_HEAD_DIM)
        qh = q_ref[:, :, sl]
        kh = k_ref[:, :, sl].astype(BF16)
        vh = v_ref[:, :, sl].astype(BF16)
        s = jnp.einsum('bqd,bkd->bqk', qh, kh, preferred_element_type=F32) * scale
        e = jnp.exp(s - jnp.max(s, axis=-1, keepdims=True))
        p = e / jnp.sum(e, axis=-1, keepdims=True)
        oh = jnp.einsum('bqk,bkd->bqd', p.astype(BF16), vh, preferred_element_type=F32)
        o_ref[:, :, sl] = oh.astype(o_ref.dtype)


def cross_attend(q, mk, mv, bb, tl):
    bsz, seq, d = q.shape
    mlen = mk.shape[1]
    return pl.pallas_call(
        _cross_kernel,
        grid=(bsz // bb, seq // tl),
        in_specs=[pl.BlockSpec((bb, tl, d), lambda i, j: (i, j, 0)),
                  pl.BlockSpec((bb, mlen, d), lambda i, j: (i, 0, 0)),
                  pl.BlockSpec((bb, mlen, d), lambda i, j: (i, 0, 0))],
        out_specs=pl.BlockSpec((bb, tl, d), lambda i, j: (i, j, 0)),
        out_shape=jax.ShapeDtypeStruct((bsz, seq, d), BF16),
        compiler_params=_cparams("parallel", "arbitrary"), name="cross_attend")(q, mk, mv)


def _ssd_kernel(xbc_ref, z_ref, dt_ref, conv0_ref, s0_ref, cw_ref, cb_ref, dtb_ref, alog_ref, dsk_ref, nw_ref,
                y_ref, sfin_ref, state_scr, xpad_scr, y_scr, *, q, lv):
    c = pl.program_id(1)
    pad = 8

    @pl.when(c == 0)
    def _():
        state_scr[...] = s0_ref[0]
        xpad_scr[0:pad, :] = jnp.zeros((pad, SSM_CONV_DIM), F32)
        xpad_scr[pad - (SSM_CONV - 1):pad, :] = conv0_ref[0]

    xpad_scr[pad:pad + q, :] = xbc_ref[0]
    conv = cb_ref[...]
    for tap in range(SSM_CONV):
        back = SSM_CONV - 1 - tap
        conv = conv + cw_ref[tap:tap + 1, :] * xpad_scr[pad - back:pad - back + q, :]
    xpad_scr[pad - (SSM_CONV - 1):pad, :] = xpad_scr[pad + q - (SSM_CONV - 1):pad + q, :]

    u = _silu(conv)
    xs = u[:, :SSM_INNER]
    bm = u[:, SSM_INNER:SSM_INNER + SSM_GROUPS * SSM_STATE].astype(BF16)
    cm = u[:, SSM_INNER + SSM_GROUPS * SSM_STATE:].astype(BF16)

    row = lax.broadcasted_iota(jnp.int32, (q, q), 0)
    col = lax.broadcasted_iota(jnp.int32, (q, q), 1)
    tri = row >= col
    dt = _softplus(dt_ref[0] + dtb_ref[...])
    if lv < q:
        dt = jnp.where(lax.broadcasted_iota(jnp.int32, dt.shape, 0) < lv, dt, 0.0)
    d_a = dt * (-jnp.exp(alog_ref[...]))
    acum = _ones_dot(tri.astype(BF16), d_a)
    acum_t = acum.T
    dt_t = dt.T
    a_last = acum_t[:, q - 1:q]
    dend_t = jnp.exp(a_last - acum_t) * dt_t
    chunk_decay = jnp.exp(a_last)
    e_acum = jnp.exp(acum)
    xs_t = xs.T

    heads_per_group = SSM_HEADS // SSM_GROUPS
    for g in range(SSM_GROUPS):
        bm_g = bm[:, g * SSM_STATE:(g + 1) * SSM_STATE]
        cm_g = cm[:, g * SSM_STATE:(g + 1) * SSM_STATE]
        cb = _dot_nt(cm_g, bm_g)
        for r in range(heads_per_group):
            h = g * heads_per_group + r
            hs = slice(h * SSM_HEAD_DIM, (h + 1) * SSM_HEAD_DIM)
            diff = acum[:, h:h + 1] - acum_t[h:h + 1, :]
            wts = cb * jnp.exp(jnp.where(tri, diff, NEG_BIG)) * dt_t[h:h + 1, :]
            s_prev = state_scr[h]
            y_h = _dot(wts.astype(BF16), xs[:, hs].astype(BF16))
            y_h = y_h + _dot_nt(cm_g, s_prev.astype(BF16)) * e_acum[:, h:h + 1]
            y_scr[:, hs] = y_h
            xw = (xs_t[hs, :] * dend_t[h:h + 1, :]).astype(BF16)
            state_scr[h] = s_prev * chunk_decay[h:h + 1, :] + _dot(xw, bm_g)

    y = y_scr[...] + dsk_ref[...] * xs
    y_ref[0] = _rms(y * _silu(z_ref[0]), nw_ref[...]).astype(y_ref.dtype)

    @pl.when(c == pl.num_programs(1) - 1)
    def _():
        sfin_ref[0] = state_scr[...]


def ssd_mixer(xbc, z, dt, conv0, s0, conv_w, conv_b, dt_bias, a_log, d_rep, norm_w, lv, out_dtype):
    bsz, seq, _ = xbc.shape
    q = SSM_CHUNK
    nc = seq // q
    row3 = lambda b, c: (b, c, 0)
    full2 = lambda b, c: (0, 0)
    return pl.pallas_call(
        functools.partial(_ssd_kernel, q=q, lv=lv),
        grid=(bsz, nc),
        in_specs=[pl.BlockSpec((1, q, SSM_CONV_DIM), row3),
                  pl.BlockSpec((1, q, SSM_INNER), row3),
                  pl.BlockSpec((1, q, LANES), row3),
                  pl.BlockSpec((1, SSM_CONV - 1, SSM_CONV_DIM), lambda b, c: (b, 0, 0)),
                  pl.BlockSpec((1, SSM_HEADS, SSM_HEAD_DIM, SSM_STATE), lambda b, c: (b, 0, 0, 0)),
                  pl.BlockSpec((SSM_CONV, SSM_CONV_DIM), full2),
                  pl.BlockSpec((1, SSM_CONV_DIM), full2),
                  pl.BlockSpec((1, LANES), full2),
                  pl.BlockSpec((1, LANES), full2),
                  pl.BlockSpec((1, SSM_INNER), full2),
                  pl.BlockSpec((1, SSM_INNER), full2)],
        out_specs=[pl.BlockSpec((1, q, SSM_INNER), row3),
                   pl.BlockSpec((1, SSM_HEADS, SSM_HEAD_DIM, SSM_STATE), lambda b, c: (b, 0, 0, 0))],
        out_shape=[jax.ShapeDtypeStruct((bsz, seq, SSM_INNER), out_dtype),
                   jax.ShapeDtypeStruct((bsz, SSM_HEADS, SSM_HEAD_DIM, SSM_STATE), F32)],
        scratch_shapes=[pltpu.VMEM((SSM_HEADS, SSM_HEAD_DIM, SSM_STATE), F32),
                        pltpu.VMEM((q + 8, SSM_CONV_DIM), F32),
                        pltpu.VMEM((q, SSM_INNER), F32)],
        compiler_params=_cparams("parallel", "arbitrary"), name="ssd_mixer",
    )(xbc, z, dt, conv0, s0, conv_w, conv_b, dt_bias, a_log, d_rep, norm_w)


def _sb_tile(q2, kblk, vblk, tmat, carry, acc, valid):
    s = _dot_nt(q2, kblk)
    t = jnp.log1p(jnp.exp(-jnp.abs(s)))
    log_beta = jnp.minimum(s, 0.0) - t
    log_keep = log_beta - s
    if valid is not None:
        log_keep = jnp.where(valid, log_keep, 0.0)
    later = _dot_ones(log_keep, tmat)
    w = jnp.exp(log_beta + later + carry)
    if valid is not None:
        w = jnp.where(valid, w, 0.0)
    acc = acc + _dot(w.astype(BF16), vblk)
    carry = carry + later[:, 0:1] + log_keep[:, 0:1]
    return carry, acc


def _later_mask(n):
    r = lax.broadcasted_iota(jnp.int32, (n, n), 0)
    c = lax.broadcasted_iota(jnp.int32, (n, n), 1)
    return (r > c).astype(BF16)


def _sb_kernel(q_ref, k_ref, v_ref, o_ref, k2_scr, v2_scr, *, tq):
    i = pl.program_id(1)
    seq = k_ref.shape[1]
    half = SB_HEAD_DIM

    @pl.when(i == 0)
    def _():
        lane = lax.broadcasted_iota(jnp.int32, (seq, LANES), 1)
        for p in range(SB_KV_HEADS // 2):
            for src, dst in ((k_ref, k2_scr), (v_ref, v2_scr)):
                pair = src[0, :, p * LANES:(p + 1) * LANES]
                swapped = pltpu.roll(pair, half, 1)
                dst[2 * p] = jnp.where(lane < half, pair, swapped).astype(BF16)
                dst[2 * p + 1] = jnp.where(lane < half, swapped, pair).astype(BF16)

    tmat = _later_mask(tq)
    lane_q = lax.broadcasted_iota(jnp.int32, (tq, LANES), 1)
    row2 = lax.broadcasted_iota(jnp.int32, (2 * tq, tq), 0)
    col2 = lax.broadcasted_iota(jnp.int32, (2 * tq, tq), 1)
    diag_valid = col2 < jnp.where(row2 >= tq, row2 - tq, row2)
    zero = jnp.zeros((), BF16)

    for kh in range(SB_KV_HEADS):
        qg = q_ref[0, :, kh * LANES:(kh + 1) * LANES] * jnp.asarray(SB_HEAD_DIM ** -0.5, BF16)
        q2 = jnp.concatenate([jnp.where(lane_q < half, qg, zero), jnp.where(lane_q >= half, qg, zero)], axis=0)

        def tile(j, carry, acc, valid):
            start = pl.multiple_of(j * tq, tq)
            return _sb_tile(q2, k2_scr[kh, pl.ds(start, tq), :], v2_scr[kh, pl.ds(start, tq), :],
                            tmat, carry, acc, valid)

        carry, acc = tile(i, jnp.zeros((2 * tq, 1), F32), jnp.zeros((2 * tq, LANES), F32), diag_valid)

        def cond(st):
            j, carry, _ = st
            return jnp.logical_and(j >= 0, jnp.max(carry) > SB_DEAD_LOG)

        def body(st):
            j, carry, acc = st
            carry, acc = tile(j, carry, acc, None)
            return j - 1, carry, acc

        _, _, acc = lax.while_loop(cond, body, (i - 1, carry, acc))
        o_ref[0, :, kh * LANES:(kh + 1) * LANES] = jnp.where(lane_q < half, acc[:tq], acc[tq:]).astype(o_ref.dtype)


def sb_attention(q, k, v, tq=128):
    bsz, seq, _ = q.shape
    return pl.pallas_call(
        functools.partial(_sb_kernel, tq=tq),
        grid=(bsz, seq // tq),
        in_specs=[pl.BlockSpec((1, tq, SB_Q_DIM), lambda b, i: (b, i, 0)),
                  pl.BlockSpec((1, seq, SB_KV_DIM), lambda b, i: (b, 0, 0)),
                  pl.BlockSpec((1, seq, SB_KV_DIM), lambda b, i: (b, 0, 0))],
        out_specs=pl.BlockSpec((1, tq, SB_Q_DIM), lambda b, i: (b, i, 0)),
        out_shape=jax.ShapeDtypeStruct((bsz, seq, SB_Q_DIM), BF16),
        scratch_shapes=[pltpu.VMEM((SB_KV_HEADS, seq, LANES), BF16),
                        pltpu.VMEM((SB_KV_HEADS, seq, LANES), BF16)],
        compiler_params=_cparams("parallel", "arbitrary"), name="sb_attention")(q, k, v)


def _sb_dec_kernel(pt_ref, q_ref, kn_ref, vn_ref, *rest, pp, lq):
    del pt_ref
    k_refs, v_refs = rest[:pp], rest[pp:2 * pp]
    o_ref, carry_scr, acc_scr = rest[2 * pp:]
    s = pl.program_id(1)
    q2 = q_ref[0]
    rows = q2.shape[0]
    tmat = _later_mask(PAGE_SIZE)

    @pl.when(s == 0)
    def _():
        r = lax.broadcasted_iota(jnp.int32, (rows, PAGE_SIZE), 0)
        cidx = lax.broadcasted_iota(jnp.int32, (rows, PAGE_SIZE), 1)
        valid = cidx < r % lq
        carry, acc = _sb_tile(q2, kn_ref[0].astype(BF16), vn_ref[0].astype(BF16), tmat,
                              jnp.zeros((rows, 1), F32), jnp.zeros((rows, SB_KV_DIM), F32), valid)
        carry_scr[...] = jnp.broadcast_to(carry, carry_scr.shape)
        acc_scr[...] = acc

    carry = carry_scr[:, 0:1]
    acc = acc_scr[...]
    for u in range(pp):
        carry, acc = _sb_tile(q2, k_refs[u][0].astype(BF16), v_refs[u][0].astype(BF16), tmat, carry, acc, None)
    carry_scr[...] = jnp.broadcast_to(carry, carry_scr.shape)
    acc_scr[...] = acc

    @pl.when(s == pl.num_programs(1) - 1)
    def _():
        o_ref[0] = acc


def _pages_per_step(n_pages, want=8):
    pp = want
    while n_pages % pp:
        pp //= 2
    return pp


def sb_decode(q_bd, k_new, v_new, k_pool, v_pool, page_table, lq):
    bsz, rows, _ = q_bd.shape
    n_pages = page_table.shape[1]
    pp = _pages_per_step(n_pages)

    def page_map(u):
        return lambda b, s, pt: (pt[b, n_pages - 1 - (s * pp + u)], 0, 0)

    per_b = lambda b, s, pt: (b, 0, 0)
    page_spec = [pl.BlockSpec((1, PAGE_SIZE, SB_KV_DIM), page_map(u)) for u in range(pp)]
    grid_spec = pltpu.PrefetchScalarGridSpec(
        num_scalar_prefetch=1, grid=(bsz, n_pages // pp),
        in_specs=[pl.BlockSpec((1, rows, SB_KV_DIM), per_b),
                  pl.BlockSpec((1, PAGE_SIZE, SB_KV_DIM), per_b),
                  pl.BlockSpec((1, PAGE_SIZE, SB_KV_DIM), per_b)] + page_spec + page_spec,
        out_specs=pl.BlockSpec((1, rows, SB_KV_DIM), per_b),
        scratch_shapes=[pltpu.VMEM((rows, LANES), F32), pltpu.VMEM((rows, SB_KV_DIM), F32)])
    return pl.pallas_call(
        functools.partial(_sb_dec_kernel, pp=pp, lq=lq), grid_spec=grid_spec,
        out_shape=jax.ShapeDtypeStruct((bsz, rows, SB_KV_DIM), F32),
        compiler_params=_cparams("parallel", "arbitrary"), name="sb_decode",
    )(page_table, q_bd, k_new, v_new, *([k_pool] * pp), *([v_pool] * pp))


def _hgrn_kernel(hq_ref, hf_ref, hi_ref, hg_ref, s0_ref, lb_ref, nw_ref, o_ref, sfin_ref, st_scr, o_scr, *, tl, lv):
    c = pl.program_id(1)
    nchunk = tl // HG_CHUNK

    @pl.when(c == 0)
    def _():
        for h in range(HG_HEADS):
            st_scr[h] = s0_ref[0, h].T

    lb = lb_ref[...]
    f_pre = hf_ref[0]
    logf = jnp.log(lb + (1.0 - lb) * _sigmoid(f_pre))
    kk = (1.0 - lb) * _sigmoid(-f_pre)
    if lv < tl:
        live = lax.broadcasted_iota(jnp.int32, logf.shape, 0) < lv
        logf = jnp.where(live, logf, 0.0)
        kk = jnp.where(live, kk, 0.0)
    row = lax.broadcasted_iota(jnp.int32, (tl, tl), 0)
    col = lax.broadcasted_iota(jnp.int32, (tl, tl), 1)
    same = (row // HG_CHUNK) == (col // HG_CHUNK)
    causal = jnp.logical_and(same, row >= col)
    bcum = _ones_dot(causal.astype(BF16), logf)
    btot = _ones_dot(same.astype(BF16), logf)
    q_dec = (hq_ref[0] * jnp.exp(bcum)).astype(BF16)
    k_inv = (kk * jnp.exp(-bcum)).astype(BF16)
    k_end = kk * jnp.exp(btot - bcum)
    e_tot = jnp.exp(btot)
    v = hi_ref[0].astype(BF16)
    rows_tl = lax.broadcasted_iota(jnp.int32, (tl, HG_DK), 0)

    for h in range(HG_HEADS):
        hs = slice(h * HG_DK, (h + 1) * HG_DK)
        att = jnp.where(causal, _dot_nt(q_dec[:, hs], k_inv[:, hs]), 0.0)
        o_scr[:, hs] = _dot(att.astype(BF16), v[:, hs])
        st = st_scr[h]
        for cc in range(nchunk):
            rs = slice(cc * HG_CHUNK, (cc + 1) * HG_CHUNK)
            o_scr[rs, hs] += _dot_nt(q_dec[rs, hs], st.astype(BF16))
            in_chunk = (rows_tl // HG_CHUNK) == cc
            k_c = jnp.where(in_chunk, k_end[:, hs], 0.0).astype(BF16)
            decay = e_tot[cc * HG_CHUNK:cc * HG_CHUNK + 1, hs]
            st = st * decay + _dot_tn(v[:, hs], k_c)
        st_scr[h] = st

    o = o_scr[...]
    gate = _silu(hg_ref[0])
    for h in range(HG_HEADS):
        hs = slice(h * HG_DV, (h + 1) * HG_DV)
        o_ref[0, :, hs] = (_rms(o[:, hs], nw_ref[:, hs]) * gate[:, hs]).astype(o_ref.dtype)

    @pl.when(c == pl.num_programs(1) - 1)
    def _():
        for h in range(HG_HEADS):
            sfin_ref[0, h] = st_scr[h].T


def hgrn_mixer(hq, hf, hi, hg, s0, lb, norm_w, tl, lv, out_dtype):
    bsz, seq, _ = hq.shape
    row3 = lambda b, c: (b, c, 0)
    full2 = lambda b, c: (0, 0)
    st4 = lambda b, c: (b, 0, 0, 0)
    return pl.pallas_call(
        functools.partial(_hgrn_kernel, tl=tl, lv=lv),
        grid=(bsz, seq // tl),
        in_specs=[pl.BlockSpec((1, tl, HG_DIM), row3)] * 4 + [
            pl.BlockSpec((1, HG_HEADS, HG_DK, HG_DV), st4),
            pl.BlockSpec((1, HG_DIM), full2), pl.BlockSpec((1, HG_DIM), full2)],
        out_specs=[pl.BlockSpec((1, tl, HG_DIM), row3), pl.BlockSpec((1, HG_HEADS, HG_DK, HG_DV), st4)],
        out_shape=[jax.ShapeDtypeStruct((bsz, seq, HG_DIM), out_dtype),
                   jax.ShapeDtypeStruct((bsz, HG_HEADS, HG_DK, HG_DV), F32)],
        scratch_shapes=[pltpu.VMEM((HG_HEADS, HG_DV, HG_DK), F32), pltpu.VMEM((tl, HG_DIM), F32)],
        compiler_params=_cparams("parallel", "arbitrary"), name="hgrn_mixer",
    )(hq, hf, hi, hg, s0, lb, norm_w)


def _mla_pre_kernel(cq_ref, ckv_ref, kra_ref, krb_ref, cosq_ref, sinq_ref, cosk_ref, sink_ref, qnw_ref, kvnw_ref,
                    wn_ref, wr_ref, wrs_ref, wukt_ref, ql_ref, qr_ref, c_ref, r_ref):
    n = _rms(cq_ref[...], qnw_ref[...]).astype(BF16)
    q_nope = _dot(n, wn_ref[...]).astype(BF16)
    q_rope = _dot(n, wr_ref[...]) * cosq_ref[...] + _dot(n, wrs_ref[...]) * sinq_ref[...]
    qr_ref[...] = q_rope.astype(qr_ref.dtype)
    for h in range(MLA_HEADS):
        ql_ref[:, h * MLA_KV_LORA:(h + 1) * MLA_KV_LORA] = _dot(
            q_nope[:, h * MLA_NOPE:(h + 1) * MLA_NOPE], wukt_ref[h]).astype(ql_ref.dtype)
    c_ref[...] = _rms(ckv_ref[...], kvnw_ref[...])
    r = kra_ref[...] * cosk_ref[...] + krb_ref[...] * sink_ref[...]
    r_ref[...] = r[:, :MLA_ROPE]


def mla_pre(cq, ckv, kra, krb, tabs, qnw, kvnw, wn, wr, wrs, wukt, tm=512):
    m = cq.shape[0]
    tm = min(tm, m, tabs[0].shape[0])
    nblk = tabs[0].shape[0] // tm
    rowb = lambda i: (i, 0)
    tabb = lambda i: (i % nblk, 0)
    full2 = lambda i: (0, 0)
    hq = MLA_HEADS * MLA_ROPE
    return pl.pallas_call(
        _mla_pre_kernel,
        grid=(m // tm,),
        in_specs=[pl.BlockSpec((tm, MLA_Q_LORA), rowb), pl.BlockSpec((tm, MLA_KV_LORA), rowb),
                  pl.BlockSpec((tm, LANES), rowb), pl.BlockSpec((tm, LANES), rowb),
                  pl.BlockSpec((tm, hq), tabb), pl.BlockSpec((tm, hq), tabb),
                  pl.BlockSpec((tm, LANES), tabb), pl.BlockSpec((tm, LANES), tabb),
                  pl.BlockSpec((1, MLA_Q_LORA), full2), pl.BlockSpec((1, MLA_KV_LORA), full2),
                  pl.BlockSpec(wn.shape, full2), pl.BlockSpec(wr.shape, full2), pl.BlockSpec(wrs.shape, full2),
                  pl.BlockSpec(wukt.shape, lambda i: (0, 0, 0))],
        out_specs=[pl.BlockSpec((tm, MLA_HEADS * MLA_KV_LORA), rowb), pl.BlockSpec((tm, hq), rowb),
                   pl.BlockSpec((tm, MLA_KV_LORA), rowb), pl.BlockSpec((tm, MLA_ROPE), rowb)],
        out_shape=[jax.ShapeDtypeStruct((m, MLA_HEADS * MLA_KV_LORA), BF16), jax.ShapeDtypeStruct((m, hq), BF16),
                   jax.ShapeDtypeStruct((m, MLA_KV_LORA), F32), jax.ShapeDtypeStruct((m, MLA_ROPE), F32)],
        compiler_params=_cparams("parallel"), name="mla_pre",
    )(cq, ckv, kra, krb, *tabs, qnw, kvnw, wn, wr, wrs, wukt)


def _mla_tile(qs, qrs, cblk, rblk, m, l, acc, valid):
    s = (_dot_nt(qs, cblk) + _dot_nt(qrs, rblk)) * MLA_SCALE
    if valid is not None:
        s = jnp.where(valid, s, NEG_BIG)
    m_new = jnp.maximum(m, jnp.max(s, axis=-1, keepdims=True))
    alpha = jnp.exp(m - m_new)
    p = jnp.exp(s - m_new)
    l = alpha * l + jnp.sum(p, axis=-1, keepdims=True)
    acc = alpha * acc + _dot(p.astype(BF16), cblk)
    return m_new, l, acc


def _mla_kernel(ql_ref, qr_ref, c_ref, r_ref, wuv_ref, o_ref, qs_scr, qrs_scr, m_scr, l_scr, acc_scr, *, tq):
    i = pl.program_id(1)
    for h in range(MLA_HEADS):
        qs_scr[h * tq:(h + 1) * tq, :] = ql_ref[0, :, h * MLA_KV_LORA:(h + 1) * MLA_KV_LORA]
        qrs_scr[h * tq:(h + 1) * tq, :] = qr_ref[0, :, h * MLA_ROPE:(h + 1) * MLA_ROPE]
    rows = MLA_HEADS * tq
    qs = qs_scr[...]
    qrs = qrs_scr[...]

    def blocks(j):
        start = pl.multiple_of(j * tq, tq)
        return c_ref[0, pl.ds(start, tq), :].astype(BF16), r_ref[0, pl.ds(start, tq), :].astype(BF16)

    r2 = lax.broadcasted_iota(jnp.int32, (rows, tq), 0)
    c2 = lax.broadcasted_iota(jnp.int32, (rows, tq), 1)
    cblk, rblk = blocks(i)
    m, l, acc = _mla_tile(qs, qrs, cblk, rblk, jnp.full((rows, 1), NEG_BIG, F32), jnp.zeros((rows, 1), F32),
                          jnp.zeros((rows, MLA_KV_LORA), F32), c2 <= r2 % tq)
    m_scr[...] = jnp.broadcast_to(m, m_scr.shape)
    l_scr[...] = jnp.broadcast_to(l, l_scr.shape)
    acc_scr[...] = acc

    def body(j, carry):
        cblk, rblk = blocks(j)
        m, l, acc = _mla_tile(qs, qrs, cblk, rblk, m_scr[:, 0:1], l_scr[:, 0:1], acc_scr[...], None)
        m_scr[...] = jnp.broadcast_to(m, m_scr.shape)
        l_scr[...] = jnp.broadcast_to(l, l_scr.shape)
        acc_scr[...] = acc
        return carry

    lax.fori_loop(0, i, body, 0)
    ctx = (acc_scr[...] / l_scr[:, 0:1]).astype(BF16)
    for h in range(MLA_HEADS):
        o_ref[0, :, h * MLA_V:(h + 1) * MLA_V] = _dot(ctx[h * tq:(h + 1) * tq], wuv_ref[h]).astype(o_ref.dtype)


def mla_attention(ql, qr, c, r, wuv, tq=128):
    bsz, seq, _ = ql.shape
    rows = MLA_HEADS * tq
    return pl.pallas_call(
        functools.partial(_mla_kernel, tq=tq),
        grid=(bsz, seq // tq),
        in_specs=[pl.BlockSpec((1, tq, MLA_HEADS * MLA_KV_LORA), lambda b, i: (b, i, 0)),
                  pl.BlockSpec((1, tq, MLA_HEADS * MLA_ROPE), lambda b, i: (b, i, 0)),
                  pl.BlockSpec((1, seq, MLA_KV_LORA), lambda b, i: (b, 0, 0)),
                  pl.BlockSpec((1, seq, MLA_ROPE), lambda b, i: (b, 0, 0)),
                  pl.BlockSpec(wuv.shape, lambda b, i: (0, 0, 0))],
        out_specs=pl.BlockSpec((1, tq, MLA_HEADS * MLA_V), lambda b, i: (b, i, 0)),
        out_shape=jax.ShapeDtypeStruct((bsz, seq, MLA_HEADS * MLA_V), BF16),
        scratch_shapes=[pltpu.VMEM((rows, MLA_KV_LORA), BF16), pltpu.VMEM((rows, MLA_ROPE), BF16),
                        pltpu.VMEM((rows, LANES), F32), pltpu.VMEM((rows, LANES), F32),
                        pltpu.VMEM((rows, MLA_KV_LORA), F32)],
        compiler_params=_cparams("parallel", "arbitrary"), name="mla_attention")(ql, qr, c, r, wuv)


def _mla_dec_kernel(pt_ref, ql_ref, qr_ref, cn_ref, rn_ref, *rest, pp, lq):
    del pt_ref
    c_refs, r_refs = rest[:pp], rest[pp:2 * pp]
    o_ref, m_scr, l_scr, acc_scr = rest[2 * pp:]
    s = pl.program_id(1)
    qs = ql_ref[0]
    qrs = qr_ref[0]
    rows = qs.shape[0]

    @pl.when(s == 0)
    def _():
        r2 = lax.broadcasted_iota(jnp.int32, (rows, PAGE_SIZE), 0)
        c2 = lax.broadcasted_iota(jnp.int32, (rows, PAGE_SIZE), 1)
        m, l, acc = _mla_tile(qs, qrs, cn_ref[0].astype(BF16), rn_ref[0].astype(BF16),
                              jnp.full((rows, 1), NEG_BIG, F32), jnp.zeros((rows, 1), F32),
                              jnp.zeros((rows, MLA_KV_LORA), F32), c2 <= r2 % lq)
        m_scr[...] = jnp.broadcast_to(m, m_scr.shape)
        l_scr[...] = jnp.broadcast_to(l, l_scr.shape)
        acc_scr[...] = acc

    m, l, acc = m_scr[:, 0:1], l_scr[:, 0:1], acc_scr[...]
    for u in range(pp):
        m, l, acc = _mla_tile(qs, qrs, c_refs[u][0].astype(BF16), r_refs[u][0].astype(BF16), m, l, acc, None)
    m_scr[...] = jnp.broadcast_to(m, m_scr.shape)
    l_scr[...] = jnp.broadcast_to(l, l_scr.shape)
    acc_scr[...] = acc

    @pl.when(s == pl.num_programs(1) - 1)
    def _():
        o_ref[0] = acc / l


def mla_decode(ql, qr, c_new, r_new, c_pool, r_pool, page_table, lq):
    bsz, rows, _ = ql.shape
    n_pages = page_table.shape[1]
    pp = _pages_per_step(n_pages)

    def page_map(u):
        return lambda b, s, pt: (pt[b, s * pp + u], 0, 0)

    per_b = lambda b, s, pt: (b, 0, 0)
    grid_spec = pltpu.PrefetchScalarGridSpec(
        num_scalar_prefetch=1, grid=(bsz, n_pages // pp),
        in_specs=[pl.BlockSpec((1, rows, MLA_KV_LORA), per_b), pl.BlockSpec((1, rows, MLA_ROPE), per_b),
                  pl.BlockSpec((1, PAGE_SIZE, MLA_KV_LORA), per_b), pl.BlockSpec((1, PAGE_SIZE, MLA_ROPE), per_b)]
        + [pl.BlockSpec((1, PAGE_SIZE, MLA_KV_LORA), page_map(u)) for u in range(pp)]
        + [pl.BlockSpec((1, PAGE_SIZE, MLA_ROPE), page_map(u)) for u in range(pp)],
        out_specs=pl.BlockSpec((1, rows, MLA_KV_LORA), per_b),
        scratch_shapes=[pltpu.VMEM((rows, LANES), F32), pltpu.VMEM((rows, LANES), F32),
                        pltpu.VMEM((rows, MLA_KV_LORA), F32)])
    return pl.pallas_call(
        functools.partial(_mla_dec_kernel, pp=pp, lq=lq), grid_spec=grid_spec,
        out_shape=jax.ShapeDtypeStruct((bsz, rows, MLA_KV_LORA), F32),
        compiler_params=_cparams("parallel", "arbitrary"), name="mla_decode",
    )(page_table, ql, qr, c_new, r_new, *([c_pool] * pp), *([r_pool] * pp))


def _head_linear_kernel(x_ref, w_ref, o_ref):
    o_ref[0] = _dot(x_ref[0].astype(BF16), w_ref[0])


def head_linear(x, w):
    nh, m, k = x.shape
    n = w.shape[2]
    return pl.pallas_call(
        _head_linear_kernel, grid=(nh,),
        in_specs=[pl.BlockSpec((1, m, k), lambda h: (h, 0, 0)), pl.BlockSpec((1, k, n), lambda h: (h, 0, 0))],
        out_specs=pl.BlockSpec((1, m, n), lambda h: (h, 0, 0)),
        out_shape=jax.ShapeDtypeStruct((nh, m, n), F32),
        compiler_params=_cparams("parallel"), name="head_linear")(x, w)


def _pad_cols(w, n):
    return jnp.pad(w, ((0, 0), (0, n - w.shape[1])))


def _lane_row(v):
    return jnp.pad(v.astype(F32), (0, LANES - v.shape[0]))[None, :]


def _rope_tables(pos):
    half = MLA_ROPE // 2
    inv = ROPE_THETA ** (-jnp.arange(half, dtype=F32) / half)
    ang = pos[:, None] * inv[None, :]
    cos = jnp.concatenate([jnp.cos(ang), jnp.cos(ang)], axis=-1)
    sin = jnp.concatenate([-jnp.sin(ang), jnp.sin(ang)], axis=-1)
    cosq, sinq = jnp.tile(cos, (1, MLA_HEADS)), jnp.tile(sin, (1, MLA_HEADS))
    cosk, sink = _pad_cols(cos, LANES), _pad_cols(sin, LANES)
    return cosq, sinq, cosk, sink


def _swap_halves(w):
    half = MLA_ROPE // 2
    g = w.reshape(w.shape[0], -1, 2, half)
    return g[:, :, ::-1, :].reshape(w.shape)


def _even_weights(w_in, w_out):
    cuts = np.cumsum([SSM_INNER, SSM_CONV_DIM, SSM_HEADS, SB_Q_DIM, SB_KV_DIM, SB_KV_DIM])[:-1].tolist()
    wz, wxbc, wdt, wq, wk, wv = jnp.split(w_in.astype(BF16), cuts, axis=1)
    wo = w_out.astype(BF16)
    return (wz, wxbc, _pad_cols(wdt, LANES), wq, wk, wv), (wo[:SSM_INNER], wo[SSM_INNER:])


def _odd_weights(w_in, w_out):
    cuts = np.cumsum([HG_DIM, HG_DIM, HG_DIM, HG_DIM, MLA_Q_LORA, MLA_KV_LORA, MLA_ROPE])[:-1].tolist()
    whq, whf, whi, whg, wcq, wckv, wkr = jnp.split(w_in.astype(BF16), cuts, axis=1)
    wo = w_out.astype(BF16)
    ws = (whq, whf, whi, whg, wcq, wckv, _pad_cols(wkr, LANES), _pad_cols(_swap_halves(wkr), LANES))
    return ws, (wo[:HG_DIM], wo[HG_DIM:])


def _pad_rows(a, n):
    return jnp.pad(a, ((0, 0), (0, n - a.shape[1]), (0, 0)))


def _even_layer(x, sample, page_table, w_in, w_out, conv_w, conv_b, dt_bias, a_log, d_skip, norm_w,
                conv0, s0, k_pool, v_pool):
    bsz, seq, d = x.shape
    ws, (wo_a, wo_b) = _even_weights(w_in, w_out)
    z, xbc, dt, q, k, v = multi_linear(x.reshape(bsz * seq, d), ws, (F32, F32, F32, BF16, F32, F32))
    xbc3, z3, dt3 = (a.reshape(bsz, seq, -1) for a in (xbc, z, dt))
    conv_new = jnp.concatenate([conv0, xbc3], axis=1)[:, -(SSM_CONV - 1):]
    lv = seq
    if seq < SSM_CHUNK:
        lv = seq
        xbc3, z3, dt3 = (_pad_rows(a, SSM_CHUNK) for a in (xbc3, z3, dt3))
    y_a, s_new = ssd_mixer(xbc3, z3, dt3, conv0, s0, conv_w, conv_b[None, :], _lane_row(dt_bias), _lane_row(a_log),
                           jnp.repeat(d_skip.astype(F32), SSM_HEAD_DIM)[None, :], norm_w[None, :],
                           min(lv, SSM_CHUNK), F32 if sample else BF16)
    y_a = y_a[:, :seq].reshape(bsz * seq, SSM_INNER)
    q3 = q.reshape(bsz, seq, SB_Q_DIM)
    k3, v3 = k.reshape(bsz, seq, SB_KV_DIM), v.reshape(bsz, seq, SB_KV_DIM)
    if sample:
        group = SB_HEADS // SB_KV_HEADS
        q5 = (q3 * jnp.asarray(SB_HEAD_DIM ** -0.5, BF16)).reshape(bsz, seq, SB_KV_HEADS, group, SB_HEAD_DIM)
        q5 = jnp.transpose(q5, (0, 2, 3, 1, 4)).reshape(bsz, SB_KV_HEADS, group * seq, SB_HEAD_DIM)
        q_bd = jnp.einsum('bkxd,kj->bkxjd', q5, jnp.eye(SB_KV_HEADS, dtype=BF16))
        q_bd = q_bd.reshape(bsz, SB_KV_HEADS * group * seq, SB_KV_DIM)
        acc = sb_decode(q_bd, _pad_rows(k3, PAGE_SIZE), _pad_rows(v3, PAGE_SIZE),
                        k_pool.reshape(k_pool.shape[0], PAGE_SIZE, SB_KV_DIM),
                        v_pool.reshape(v_pool.shape[0], PAGE_SIZE, SB_KV_DIM), page_table, seq)
        acc = acc.reshape(bsz, SB_KV_HEADS, group, seq, SB_KV_HEADS, SB_HEAD_DIM)
        y_b = jnp.einsum('bkrtkd->btkrd', acc).reshape(bsz * seq, SB_Q_DIM)
    else:
        y_b = sb_attention(q3, k3, v3).reshape(bsz * seq, SB_Q_DIM)
    return (y_a, y_b), (wo_a, wo_b), s_new, conv_new, k3.reshape(bsz, seq, SB_KV_HEADS, SB_HEAD_DIM), \
        v3.reshape(bsz, seq, SB_KV_HEADS, SB_HEAD_DIM)


def _odd_layer(x, sample, pos_start, page_table, lb, w_in, w_out, hg_norm_w, q_norm_w, kv_norm_w, w_uq, w_uk, w_uv,
               s0, c_pool, r_pool):
    bsz, seq, d = x.shape
    m = bsz * seq
    ws, (wo_c, wo_d) = _odd_weights(w_in, w_out)
    hq, hf, hi, hg, cq, ckv, kra, krb = multi_linear(x.reshape(m, d), ws, (F32,) * 8)
    h3 = [a.reshape(bsz, seq, HG_DIM) for a in (hq, hf, hi, hg)]
    tl = min(256, seq)
    lv = tl
    if seq < HG_CHUNK:
        tl, lv = HG_CHUNK, seq
        h3 = [_pad_rows(a, HG_CHUNK) for a in h3]
    o_c, s_new = hgrn_mixer(*h3, s0, lb[None, :].astype(F32), hg_norm_w.reshape(1, HG_DIM), tl, lv,
                            F32 if sample else BF16)
    o_c = o_c[:, :seq].reshape(m, HG_DIM)
    pos = jnp.arange(seq, dtype=F32) + pos_start
    tabs = _rope_tables(pos)
    if sample:
        tabs = tuple(jnp.tile(t, (bsz, 1)) for t in tabs)
    w_uq_b = w_uq.astype(BF16)
    wn = w_uq_b[:, :, :MLA_NOPE].reshape(MLA_Q_LORA, MLA_HEADS * MLA_NOPE)
    wr = w_uq_b[:, :, MLA_NOPE:].reshape(MLA_Q_LORA, MLA_HEADS * MLA_ROPE)
    wukt = jnp.transpose(w_uk.astype(BF16), (1, 2, 0))
    wuv = jnp.transpose(w_uv.astype(BF16), (1, 0, 2))
    ql, qr, c_new, r_new = mla_pre(cq, ckv, kra, krb, tabs, q_norm_w[None, :], kv_norm_w[None, :],
                                   wn, wr, _swap_halves(wr), wukt)
    c3, r3 = c_new.reshape(bsz, seq, MLA_KV_LORA), r_new.reshape(bsz, seq, MLA_ROPE)
    if sample:
        ql3 = jnp.transpose(ql.reshape(bsz, seq, MLA_HEADS, MLA_KV_LORA), (0, 2, 1, 3))
        qr3 = jnp.transpose(qr.reshape(bsz, seq, MLA_HEADS, MLA_ROPE), (0, 2, 1, 3))
        ctx = mla_decode(ql3.reshape(bsz, MLA_HEADS * seq, MLA_KV_LORA), qr3.reshape(bsz, MLA_HEADS * seq, MLA_ROPE),
                         _pad_rows(c3, PAGE_SIZE), _pad_rows(r3, PAGE_SIZE), c_pool, r_pool, page_table, seq)
        ctx = jnp.transpose(ctx.reshape(bsz, MLA_HEADS, seq, MLA_KV_LORA), (1, 0, 2, 3))
        o_d = head_linear(ctx.reshape(MLA_HEADS, m, MLA_KV_LORA), wuv)
        o_d = jnp.transpose(o_d, (1, 0, 2)).reshape(m, MLA_HEADS * MLA_V)
    else:
        o_d = mla_attention(ql.reshape(bsz, seq, -1), qr.reshape(bsz, seq, -1), c3, r3, wuv)
        o_d = o_d.reshape(m, MLA_HEADS * MLA_V)
    return (o_c, o_d), (wo_c, wo_d), s_new, c3, r3


def _hgrn_lower_bound(gamma, layer):
    p = jax.nn.softmax(gamma.astype(F32), axis=0)
    return jnp.cumsum(p, axis=0)[layer] - p[0]


def kernel(x_prompt, x_sample, mem_prompt, state_ssm, state_conv, cache_sb_k, cache_sb_v, state_hgrn,
           cache_mla_latent, cache_mla_rope, cache_mem_k, cache_mem_v, page_table,
           w_in_even, w_out_even, ssm_conv_w, ssm_conv_b, ssm_dt_bias, ssm_a_log, ssm_d, ssm_norm_w,
           w_in_odd, w_out_odd, hg_lower_bound, hg_norm_w, mla_q_norm_w, mla_kv_norm_w, mla_w_uq, mla_w_uk, mla_w_uv,
           w_mem_q, w_mem_k, w_mem_v, w_mem_o, ffn_w_gate, ffn_w_up, ffn_w_down, ln_g, ln_b):
    past_len = page_table.shape[1] * PAGE_SIZE
    mem_len = mem_prompt.shape[1]

    def run(x, sample):
        bsz, seq, d = x.shape
        m = bsz * seq
        pos_start = past_len if sample else 0
        outs = dict(ssm=[], conv=[], sbk=[], sbv=[], hg=[], lat=[], rope=[], mk=[], mv=[])
        x2 = x.reshape(m, d)
        for l in range(DEPTH):
            i = l // 2
            if l % 2 == 0:
                if sample:
                    conv0, s0 = state_conv[i], state_ssm[i]
                else:
                    conv0 = jnp.zeros((bsz, SSM_CONV - 1, SSM_CONV_DIM), F32)
                    s0 = jnp.zeros((bsz, SSM_HEADS, SSM_HEAD_DIM, SSM_STATE), F32)
                ys, wos, s1, conv1, k_new, v_new = _even_layer(
                    x2.reshape(bsz, seq, d), sample, page_table, w_in_even[i], w_out_even[i], ssm_conv_w[i],
                    ssm_conv_b[i], ssm_dt_bias[i], ssm_a_log[i], ssm_d[i], ssm_norm_w[i], conv0, s0,
                    cache_sb_k[i], cache_sb_v[i])
                outs['ssm'].append(s1)
                outs['conv'].append(conv1)
                outs['sbk'].append(k_new)
                outs['sbv'].append(v_new)
            else:
                s0 = state_hgrn[i] if sample else jnp.zeros((bsz, HG_HEADS, HG_DK, HG_DV), F32)
                ys, wos, s1, c_new, r_new = _odd_layer(
                    x2.reshape(bsz, seq, d), sample, pos_start, page_table, _hgrn_lower_bound(hg_lower_bound, l),
                    w_in_odd[i], w_out_odd[i], hg_norm_w[i], mla_q_norm_w[i], mla_kv_norm_w[i], mla_w_uq[i],
                    mla_w_uk[i], mla_w_uv[i], s0, cache_mla_latent[i], cache_mla_rope[i])
                outs['hg'].append(s1)
                outs['lat'].append(c_new)
                outs['rope'].append(r_new)
            x2 = proj_ln(list(ys), list(wos), x2, ln_g[l, 0][None, :], ln_b[l, 0][None, :])
            if sample:
                mk = cache_mem_k[l].reshape(bsz, mem_len, MEM_INNER)
                mv = cache_mem_v[l].reshape(bsz, mem_len, MEM_INNER)
                bb, tl = 8, seq
            else:
                mk, mv = multi_linear(mem_prompt.reshape(bsz * mem_len, d),
                                      (w_mem_k[l].astype(BF16), w_mem_v[l].astype(BF16)), (F32, F32))
                mk, mv = mk.reshape(bsz, mem_len, MEM_INNER), mv.reshape(bsz, mem_len, MEM_INNER)
                outs['mk'].append(mk.reshape(bsz, mem_len, MEM_HEADS, MEM_HEAD_DIM))
                outs['mv'].append(mv.reshape(bsz, mem_len, MEM_HEADS, MEM_HEAD_DIM))
                bb, tl = 1, min(1024, seq)
            (q,) = multi_linear(x2, (w_mem_q[l].astype(BF16),), (BF16,))
            o = cross_attend(q.reshape(bsz, seq, MEM_INNER), mk, mv, bb, tl).reshape(m, MEM_INNER)
            x2 = proj_ln([o], [w_mem_o[l].astype(BF16)], x2, ln_g[l, 1][None, :], ln_b[l, 1][None, :])
            x2 = ffn_ln(x2, ffn_w_gate[l].astype(BF16), ffn_w_up[l].astype(BF16), ffn_w_down[l].astype(BF16),
                        ln_g[l, 2][None, :], ln_b[l, 2][None, :])
        stk = lambda a: jnp.stack(a, axis=0)
        return x2.reshape(bsz, seq, d), {k_: (stk(v_) if v_ else None) for k_, v_ in outs.items()}

    y_prompt, p = run(x_prompt, False)
    y_sample, s = run(x_sample, True)
    return (y_prompt, y_sample, p['ssm'], p['conv'], p['sbk'], p['sbv'], p['hg'], p['lat'], p['rope'], p['mk'],
            p['mv'], s['ssm'], s['conv'], s['sbk'], s['sbv'], s['hg'], s['lat'], s['rope'])
```

```python
import functools

import numpy as np
import jax
import jax.numpy as jnp
from jax import lax
from jax.experimental import pallas as pl
from jax.experimental.pallas import tpu as pltpu

F32 = jnp.float32
BF16 = jnp.bfloat16

D_MODEL = 1024
DEPTH = 2
PAGE_SIZE = 128

SSM_HEAD_DIM = 64
SSM_HEADS = 16
SSM_INNER = SSM_HEADS * SSM_HEAD_DIM
SSM_GROUPS = 2
SSM_STATE = 128
SSM_CONV = 4
SSM_CONV_DIM = SSM_INNER + 2 * SSM_GROUPS * SSM_STATE
SSM_CHUNK = 128

SB_HEADS = 8
SB_KV_HEADS = 4
SB_HEAD_DIM = 64
SB_Q_DIM = SB_HEADS * SB_HEAD_DIM
SB_KV_DIM = SB_KV_HEADS * SB_HEAD_DIM

HG_HEADS = 4
HG_DK = 128
HG_DV = 128
HG_CHUNK = 32
HG_DIM = HG_HEADS * HG_DK

MLA_HEADS = 8
MLA_Q_LORA = 384
MLA_KV_LORA = 256
MLA_NOPE = 64
MLA_ROPE = 32
MLA_V = 64
MLA_SCALE = (MLA_NOPE + MLA_ROPE) ** -0.5
ROPE_THETA = 10000.0

MEM_HEADS = 4
MEM_HEAD_DIM = 128
MEM_INNER = MEM_HEADS * MEM_HEAD_DIM

DEEPNORM_ALPHA = (2 * DEPTH) ** 0.25

LANES = 128
VMEM_LIMIT_BYTES = 48 * 1024 * 1024
NEG_BIG = -1e30
SB_DEAD_LOG = -120.0


def _cparams(*sem):
    return pltpu.CompilerParams(dimension_semantics=sem, vmem_limit_bytes=VMEM_LIMIT_BYTES)


def _sigmoid(x):
    return 1.0 / (1.0 + jnp.exp(-x))


def _silu(x):
    return x * _sigmoid(x)


def _softplus(x):
    return jnp.maximum(x, 0.0) + jnp.log1p(jnp.exp(-jnp.abs(x)))


def _dot(a, b):
    return jnp.dot(a, b, preferred_element_type=F32)


def _dot_nt(a, b):
    return lax.dot_general(a, b, (((1,), (1,)), ((), ())), preferred_element_type=F32)


def _dot_tn(a, b):
    return lax.dot_general(a, b, (((0,), (0,)), ((), ())), preferred_element_type=F32)


def _split3(x):
    hi = x.astype(BF16)
    r1 = x - hi.astype(F32)
    mid = r1.astype(BF16)
    lo = (r1 - mid.astype(F32)).astype(BF16)
    return hi, mid, lo


def _ones_dot(mask01, x):
    hi, mid, lo = _split3(x)
    return _dot(mask01, hi) + _dot(mask01, mid) + _dot(mask01, lo)


def _dot_ones(x, mask01):
    hi, mid, lo = _split3(x)
    return _dot(hi, mask01) + _dot(mid, mask01) + _dot(lo, mask01)


def _layer_norm(x, g, b):
    xc = x - jnp.mean(x, axis=-1, keepdims=True)
    var = jnp.mean(xc * xc, axis=-1, keepdims=True)
    return xc * lax.rsqrt(var + 1e-5) * g + b


def _rms(x, w, eps=1e-6):
    return x * lax.rsqrt(jnp.mean(x * x, axis=-1, keepdims=True) + eps) * w


def _multi_linear_kernel(x_ref, *refs, n):
    x = x_ref[...].astype(BF16)
    for w_ref, o_ref in zip(refs[:n], refs[n:]):
        o_ref[...] = _dot(x, w_ref[...]).astype(o_ref.dtype)


def multi_linear(x, ws, out_dtypes, tm=512):
    m, k = x.shape
    tm = min(tm, m)
    n = len(ws)
    in_specs = [pl.BlockSpec((tm, k), lambda i: (i, 0))]
    in_specs += [pl.BlockSpec(w.shape, lambda i: (0, 0)) for w in ws]
    out_specs = [pl.BlockSpec((tm, w.shape[1]), lambda i: (i, 0)) for w in ws]
    out_shape = [jax.ShapeDtypeStruct((m, w.shape[1]), dt) for w, dt in zip(ws, out_dtypes)]
    return pl.pallas_call(
        functools.partial(_multi_linear_kernel, n=n),
        grid=(m // tm,), in_specs=in_specs, out_specs=out_specs, out_shape=out_shape,
        compiler_params=_cparams("parallel"), name="multi_linear")(x, *ws)


def _proj_ln_kernel(*refs, n):
    a_refs, w_refs = refs[:n], refs[n:2 * n]
    r_ref, g_ref, b_ref, o_ref = refs[2 * n:]
    y = _dot(a_refs[0][...].astype(BF16), w_refs[0][...])
    for a_ref, w_ref in zip(a_refs[1:], w_refs[1:]):
        y = y + _dot(a_ref[...].astype(BF16), w_ref[...])
    o_ref[...] = _layer_norm(DEEPNORM_ALPHA * r_ref[...] + y, g_ref[...], b_ref[...])


def proj_ln(a_list, w_list, resid, g, b, tm=512):
    m, d = resid.shape
    tm = min(tm, m)
    n = len(a_list)
    in_specs = [pl.BlockSpec((tm, a.shape[1]), lambda i: (i, 0)) for a in a_list]
    in_specs += [pl.BlockSpec(w.shape, lambda i: (0, 0)) for w in w_list]
    in_specs += [pl.BlockSpec((tm, d), lambda i: (i, 0)),
                 pl.BlockSpec((1, d), lambda i: (0, 0)), pl.BlockSpec((1, d), lambda i: (0, 0))]
    return pl.pallas_call(
        functools.partial(_proj_ln_kernel, n=n),
        grid=(m // tm,), in_specs=in_specs, out_specs=pl.BlockSpec((tm, d), lambda i: (i, 0)),
        out_shape=jax.ShapeDtypeStruct((m, d), F32),
        compiler_params=_cparams("parallel"), name="proj_ln")(*a_list, *w_list, resid, g, b)


def _ffn_kernel(x_ref, wg_ref, wu_ref, wd_ref, g_ref, b_ref, o_ref, acc_ref):
    j = pl.program_id(1)
    x = x_ref[...]
    xb = x.astype(BF16)
    gate = _dot(xb, wg_ref[...])
    up = _dot(xb, wu_ref[...])
    part = _dot((_silu(gate) * up).astype(BF16), wd_ref[...])

    @pl.when(j == 0)
    def _():
        acc_ref[...] = part

    @pl.when(j > 0)
    def _():
        acc_ref[...] += part

    @pl.when(j == pl.num_programs(1) - 1)
    def _():
        o_ref[...] = _layer_norm(DEEPNORM_ALPHA * x + acc_ref[...], g_ref[...], b_ref[...])


def ffn_ln(x, wg, wu, wd, g, b, tm=512, th=1408):
    m, d = x.shape
    hdim = wg.shape[1]
    tm = min(tm, m)
    return pl.pallas_call(
        _ffn_kernel,
        grid=(m // tm, hdim // th),
        in_specs=[pl.BlockSpec((tm, d), lambda i, j: (i, 0)),
                  pl.BlockSpec((d, th), lambda i, j: (0, j)),
                  pl.BlockSpec((d, th), lambda i, j: (0, j)),
                  pl.BlockSpec((th, d), lambda i, j: (j, 0)),
                  pl.BlockSpec((1, d), lambda i, j: (0, 0)),
                  pl.BlockSpec((1, d), lambda i, j: (0, 0))],
        out_specs=pl.BlockSpec((tm, d), lambda i, j: (i, 0)),
        out_shape=jax.ShapeDtypeStruct((m, d), F32),
        scratch_shapes=[pltpu.VMEM((tm, d), F32)],
        compiler_params=_cparams("parallel", "arbitrary"), name="ffn_ln")(x, wg, wu, wd, g, b)


def _cross_kernel(q_ref, k_ref, v_ref, o_ref):
    scale = MEM_HEAD_DIM ** -0.5
    for h in range(MEM_HEADS):
        sl = slice(h * MEM_HEAD_DIM, (h + 1) * MEM_HEAD_DIM)
        qh = q_ref[:, :, sl]
        kh = k_ref[:, :, sl].astype(BF16)
        vh = v_ref[:, :, sl].astype(BF16)
        s = jnp.einsum('bqd,bkd->bqk', qh, kh, preferred_element_type=F32) * scale
        e = jnp.exp(s - jnp.max(s, axis=-1, keepdims=True))
        p = e / jnp.sum(e, axis=-1, keepdims=True)
        oh = jnp.einsum('bqk,bkd->bqd', p.astype(BF16), vh, preferred_element_type=F32)
        o_ref[:, :, sl] = oh.astype(o_ref.dtype)


def cross_attend(q, mk, mv, bb, tl):
    bsz, seq, d = q.shape
    mlen = mk.shape[1]
    return pl.pallas_call(
        _cross_kernel,
        grid=(bsz // bb, seq // tl),
        in_specs=[pl.BlockSpec((bb, tl, d), lambda i, j: (i, j, 0)),
                  pl.BlockSpec((bb, mlen, d), lambda i, j: (i, 0, 0)),
                  pl.BlockSpec((bb, mlen, d), lambda i, j: (i, 0, 0))],
        out_specs=pl.BlockSpec((bb, tl, d), lambda i, j: (i, j, 0)),
        out_shape=jax.ShapeDtypeStruct((bsz, seq, d), BF16),
        compiler_params=_cparams("parallel", "arbitrary"), name="cross_attend")(q, mk, mv)


def _ssd_kernel(xbc_ref, z_ref, dt_ref, conv0_ref, s0_ref, cw_ref, cb_ref, dtb_ref, alog_ref, dsk_ref, nw_ref,
                y_ref, sfin_ref, state_scr, xpad_scr, y_scr, *, q, lv):
    c = pl.program_id(1)
    pad = 8

    @pl.when(c == 0)
    def _():
        state_scr[...] = s0_ref[0]
        xpad_scr[0:pad, :] = jnp.zeros((pad, SSM_CONV_DIM), F32)
        xpad_scr[pad - (SSM_CONV - 1):pad, :] = conv0_ref[0]

    xpad_scr[pad:pad + q, :] = xbc_ref[0]
    conv = cb_ref[...]
    for tap in range(SSM_CONV):
        back = SSM_CONV - 1 - tap
        conv = conv + cw_ref[tap:tap + 1, :] * xpad_scr[pad - back:pad - back + q, :]
    xpad_scr[pad - (SSM_CONV - 1):pad, :] = xpad_scr[pad + q - (SSM_CONV - 1):pad + q, :]

    u = _silu(conv)
    xs = u[:, :SSM_INNER]
    bm = u[:, SSM_INNER:SSM_INNER + SSM_GROUPS * SSM_STATE].astype(BF16)
    cm = u[:, SSM_INNER + SSM_GROUPS * SSM_STATE:].astype(BF16)

    row = lax.broadcasted_iota(jnp.int32, (q, q), 0)
    col = lax.broadcasted_iota(jnp.int32, (q, q), 1)
    tri = row >= col
    dt = _softplus(dt_ref[0] + dtb_ref[...])
    if lv < q:
        dt = jnp.where(lax.broadcasted_iota(jnp.int32, dt.shape, 0) < lv, dt, 0.0)
    d_a = dt * (-jnp.exp(alog_ref[...]))
    acum = _ones_dot(tri.astype(BF16), d_a)
    acum_t = acum.T
    dt_t = dt.T
    a_last = acum_t[:, q - 1:q]
    dend_t = jnp.exp(a_last - acum_t) * dt_t
    chunk_decay = jnp.exp(a_last)
    e_acum = jnp.exp(acum)
    xs_t = xs.T

    heads_per_group = SSM_HEADS // SSM_GROUPS
    for g in range(SSM_GROUPS):
        bm_g = bm[:, g * SSM_STATE:(g + 1) * SSM_STATE]
        cm_g = cm[:, g * SSM_STATE:(g + 1) * SSM_STATE]
        cb = _dot_nt(cm_g, bm_g)
        for r in range(heads_per_group):
            h = g * heads_per_group + r
            hs = slice(h * SSM_HEAD_DIM, (h + 1) * SSM_HEAD_DIM)
            diff = acum[:, h:h + 1] - acum_t[h:h + 1, :]
            wts = cb * jnp.exp(jnp.where(tri, diff, NEG_BIG)) * dt_t[h:h + 1, :]
            s_prev = state_scr[h]
            y_h = _dot(wts.astype(BF16), xs[:, hs].astype(BF16))
            y_h = y_h + _dot_nt(cm_g, s_prev.astype(BF16)) * e_acum[:, h:h + 1]
            y_scr[:, hs] = y_h
            xw = (xs_t[hs, :] * dend_t[h:h + 1, :]).astype(BF16)
            state_scr[h] = s_prev * chunk_decay[h:h + 1, :] + _dot(xw, bm_g)

    y = y_scr[...] + dsk_ref[...] * xs
    y_ref[0] = _rms(y * _silu(z_ref[0]), nw_ref[...]).astype(y_ref.dtype)

    @pl.when(c == pl.num_programs(1) - 1)
    def _():
        sfin_ref[0] = state_scr[...]


def ssd_mixer(xbc, z, dt, conv0, s0, conv_w, conv_b, dt_bias, a_log, d_rep, norm_w, lv, out_dtype):
    bsz, seq, _ = xbc.shape
    q = SSM_CHUNK
    nc = seq // q
    row3 = lambda b, c: (b, c, 0)
    full2 = lambda b, c: (0, 0)
    return pl.pallas_call(
        functools.partial(_ssd_kernel, q=q, lv=lv),
        grid=(bsz, nc),
        in_specs=[pl.BlockSpec((1, q, SSM_CONV_DIM), row3),
                  pl.BlockSpec((1, q, SSM_INNER), row3),
                  pl.BlockSpec((1, q, LANES), row3),
                  pl.BlockSpec((1, SSM_CONV - 1, SSM_CONV_DIM), lambda b, c: (b, 0, 0)),
                  pl.BlockSpec((1, SSM_HEADS, SSM_HEAD_DIM, SSM_STATE), lambda b, c: (b, 0, 0, 0)),
                  pl.BlockSpec((SSM_CONV, SSM_CONV_DIM), full2),
                  pl.BlockSpec((1, SSM_CONV_DIM), full2),
                  pl.BlockSpec((1, LANES), full2),
                  pl.BlockSpec((1, LANES), full2),
                  pl.BlockSpec((1, SSM_INNER), full2),
                  pl.BlockSpec((1, SSM_INNER), full2)],
        out_specs=[pl.BlockSpec((1, q, SSM_INNER), row3),
                   pl.BlockSpec((1, SSM_HEADS, SSM_HEAD_DIM, SSM_STATE), lambda b, c: (b, 0, 0, 0))],
        out_shape=[jax.ShapeDtypeStruct((bsz, seq, SSM_INNER), out_dtype),
                   jax.ShapeDtypeStruct((bsz, SSM_HEADS, SSM_HEAD_DIM, SSM_STATE), F32)],
        scratch_shapes=[pltpu.VMEM((SSM_HEADS, SSM_HEAD_DIM, SSM_STATE), F32),
                        pltpu.VMEM((q + 8, SSM_CONV_DIM), F32),
                        pltpu.VMEM((q, SSM_INNER), F32)],
        compiler_params=_cparams("parallel", "arbitrary"), name="ssd_mixer",
    )(xbc, z, dt, conv0, s0, conv_w, conv_b, dt_bias, a_log, d_rep, norm_w)


def _later_mask(n):
    r = lax.broadcasted_iota(jnp.int32, (n, n), 0)
    c = lax.broadcasted_iota(jnp.int32, (n, n), 1)
    return (r > c).astype(BF16)


def _sb_logs(s, valid):
    t = jnp.log1p(jnp.exp(-jnp.abs(s)))
    log_beta = jnp.minimum(s, 0.0) - t
    log_keep = log_beta - s
    if valid is not None:
        log_keep = jnp.where(valid, log_keep, 0.0)
    return log_beta, log_keep


def _sb_kernel(q_ref, k_ref, v_ref, o_ref, k2_scr, v2_scr, q2_scr, carry_scr, acc_scr, *, tq):
    i = pl.program_id(1)
    seq = k_ref.shape[1]
    half = SB_HEAD_DIM
    hrows = 2 * tq
    rows = SB_KV_HEADS * hrows

    @pl.when(i == 0)
    def _():
        lane = lax.broadcasted_iota(jnp.int32, (seq, LANES), 1)
        for p in range(SB_KV_HEADS // 2):
            for src, dst in ((k_ref, k2_scr), (v_ref, v2_scr)):
                pair = src[0, :, p * LANES:(p + 1) * LANES]
                swapped = pltpu.roll(pair, half, 1)
                dst[2 * p] = jnp.where(lane < half, pair, swapped).astype(BF16)
                dst[2 * p + 1] = jnp.where(lane < half, swapped, pair).astype(BF16)

    lane_q = lax.broadcasted_iota(jnp.int32, (tq, LANES), 1)
    zero = jnp.zeros((), BF16)
    for kh in range(SB_KV_HEADS):
        qg = q_ref[0, :, kh * LANES:(kh + 1) * LANES] * jnp.asarray(SB_HEAD_DIM ** -0.5, BF16)
        q2_scr[kh * hrows:kh * hrows + tq, :] = jnp.where(lane_q < half, qg, zero)
        q2_scr[kh * hrows + tq:(kh + 1) * hrows, :] = jnp.where(lane_q >= half, qg, zero)

    tmat = _later_mask(tq)

    def tile(j, carry, acc, valid):
        start = pl.multiple_of(j * tq, tq)
        s = jnp.concatenate([_dot_nt(q2_scr[kh * hrows:(kh + 1) * hrows, :], k2_scr[kh, pl.ds(start, tq), :])
                             for kh in range(SB_KV_HEADS)], axis=0)
        log_beta, log_keep = _sb_logs(s, valid)
        w = jnp.exp(log_beta + _dot_ones(log_keep, tmat) + carry)
        if valid is not None:
            w = jnp.where(valid, w, 0.0)
        w = w.astype(BF16)
        pv = jnp.concatenate([_dot(w[kh * hrows:(kh + 1) * hrows], v2_scr[kh, pl.ds(start, tq), :])
                              for kh in range(SB_KV_HEADS)], axis=0)
        return carry + jnp.sum(log_keep, axis=-1, keepdims=True), acc + pv

    row = lax.broadcasted_iota(jnp.int32, (rows, tq), 0)
    col = lax.broadcasted_iota(jnp.int32, (rows, tq), 1)
    carry, acc = tile(i, jnp.zeros((rows, LANES), F32), jnp.zeros((rows, LANES), F32), col < row % tq)
    carry_scr[...] = carry
    acc_scr[...] = acc

    def cond(st):
        j, live = st
        return jnp.logical_and(j >= 0, live > SB_DEAD_LOG)

    def body(st):
        j, _ = st
        carry, acc = tile(j, carry_scr[...], acc_scr[...], None)
        carry_scr[...] = carry
        acc_scr[...] = acc
        return j - 1, jnp.max(carry)

    lax.while_loop(cond, body, (i - 1, jnp.max(carry)))
    for kh in range(SB_KV_HEADS):
        o_ref[0, :, kh * LANES:(kh + 1) * LANES] = jnp.where(
            lane_q < half, acc_scr[kh * hrows:kh * hrows + tq, :],
            acc_scr[kh * hrows + tq:(kh + 1) * hrows, :]).astype(o_ref.dtype)


def sb_attention(q, k, v, tq=128):
    bsz, seq, _ = q.shape
    rows = SB_KV_HEADS * 2 * tq
    return pl.pallas_call(
        functools.partial(_sb_kernel, tq=tq),
        grid=(bsz, seq // tq),
        in_specs=[pl.BlockSpec((1, tq, SB_Q_DIM), lambda b, i: (b, i, 0)),
                  pl.BlockSpec((1, seq, SB_KV_DIM), lambda b, i: (b, 0, 0)),
                  pl.BlockSpec((1, seq, SB_KV_DIM), lambda b, i: (b, 0, 0))],
        out_specs=pl.BlockSpec((1, tq, SB_Q_DIM), lambda b, i: (b, i, 0)),
        out_shape=jax.ShapeDtypeStruct((bsz, seq, SB_Q_DIM), BF16),
        scratch_shapes=[pltpu.VMEM((SB_KV_HEADS, seq, LANES), BF16),
                        pltpu.VMEM((SB_KV_HEADS, seq, LANES), BF16),
                        pltpu.VMEM((rows, LANES), BF16),
                        pltpu.VMEM((rows, LANES), F32),
                        pltpu.VMEM((rows, LANES), F32)],
        compiler_params=_cparams("parallel", "arbitrary"), name="sb_attention")(q, k, v)


def _sb_pages(q2, kts, vts, tmat, carry, acc, valid):
    rows = q2.shape[0]
    s = jnp.concatenate([_dot(q2, kt) for kt in kts], axis=0)
    log_beta, log_keep = _sb_logs(s, valid)
    later = _dot_ones(log_keep, tmat)
    total = jnp.sum(log_keep, axis=-1, keepdims=True)
    carries = []
    for u in range(len(kts)):
        carries.append(carry)
        carry = carry + total[u * rows:(u + 1) * rows]
    w = jnp.exp(log_beta + later + jnp.concatenate(carries, axis=0))
    if valid is not None:
        w = jnp.where(valid, w, 0.0)
    w = w.astype(BF16)
    for u, vt in enumerate(vts):
        acc = acc + _dot_nt(w[u * rows:(u + 1) * rows], vt)
    return carry, acc


def _sb_dec_kernel(pt_ref, q_ref, kn_ref, vn_ref, *rest, pp, lq):
    del pt_ref
    k_refs, v_refs = rest[:pp], rest[pp:2 * pp]
    o_ref, carry_scr, acc_scr = rest[2 * pp:]
    s = pl.program_id(1)
    q2 = q_ref[0]
    rows = q2.shape[0]
    tmat = _later_mask(PAGE_SIZE)

    @pl.when(s == 0)
    def _():
        r = lax.broadcasted_iota(jnp.int32, (rows, PAGE_SIZE), 0)
        cidx = lax.broadcasted_iota(jnp.int32, (rows, PAGE_SIZE), 1)
        valid = cidx < r % lq
        carry, acc = _sb_pages(q2, [kn_ref[0].astype(BF16)], [vn_ref[0].astype(BF16)], tmat,
                               jnp.zeros((rows, LANES), F32), jnp.zeros((rows, SB_KV_DIM), F32), valid)
        carry_scr[...] = carry
        acc_scr[...] = acc

    carry, acc = _sb_pages(q2, [r[0].astype(BF16) for r in k_refs], [r[0].astype(BF16) for r in v_refs], tmat,
                           carry_scr[...], acc_scr[...], None)
    carry_scr[...] = carry
    acc_scr[...] = acc

    @pl.when(s == pl.num_programs(1) - 1)
    def _():
        o_ref[0] = acc


def _pages_per_step(n_pages, want=16):
    pp = want
    while n_pages % pp:
        pp //= 2
    return pp


def sb_decode(q_bd, kt_new, vt_new, kt_pool, vt_pool, page_table, lq):
    bsz, rows, _ = q_bd.shape
    n_pages = page_table.shape[1]
    pp = _pages_per_step(n_pages)

    def page_map(u):
        return lambda b, s, pt: (pt[b, n_pages - 1 - (s * pp + u)], 0, 0)

    per_b = lambda b, s, pt: (b, 0, 0)
    page_spec = [pl.BlockSpec((1, SB_KV_DIM, PAGE_SIZE), page_map(u)) for u in range(pp)]
    grid_spec = pltpu.PrefetchScalarGridSpec(
        num_scalar_prefetch=1, grid=(bsz, n_pages // pp),
        in_specs=[pl.BlockSpec((1, rows, SB_KV_DIM), per_b),
                  pl.BlockSpec((1, SB_KV_DIM, PAGE_SIZE), per_b),
                  pl.BlockSpec((1, SB_KV_DIM, PAGE_SIZE), per_b)] + page_spec + page_spec,
        out_specs=pl.BlockSpec((1, rows, SB_KV_DIM), per_b),
        scratch_shapes=[pltpu.VMEM((rows, LANES), F32), pltpu.VMEM((rows, SB_KV_DIM), F32)])
    return pl.pallas_call(
        functools.partial(_sb_dec_kernel, pp=pp, lq=lq), grid_spec=grid_spec,
        out_shape=jax.ShapeDtypeStruct((bsz, rows, SB_KV_DIM), F32),
        compiler_params=_cparams("parallel", "arbitrary"), name="sb_decode",
    )(page_table, q_bd, kt_new, vt_new, *([kt_pool] * pp), *([vt_pool] * pp))


def _hgrn_kernel(hq_ref, hf_ref, hi_ref, hg_ref, s0_ref, lb_ref, nw_ref, o_ref, sfin_ref, st_scr, o_scr, *, tl, lv):
    c = pl.program_id(1)
    nchunk = tl // HG_CHUNK

    @pl.when(c == 0)
    def _():
        for h in range(HG_HEADS):
            st_scr[h] = s0_ref[0, h].T

    lb = lb_ref[...]
    f_pre = hf_ref[0]
    logf = jnp.log(lb + (1.0 - lb) * _sigmoid(f_pre))
    kk = (1.0 - lb) * _sigmoid(-f_pre)
    if lv < tl:
        live = lax.broadcasted_iota(jnp.int32, logf.shape, 0) < lv
        logf = jnp.where(live, logf, 0.0)
        kk = jnp.where(live, kk, 0.0)
    row = lax.broadcasted_iota(jnp.int32, (tl, tl), 0)
    col = lax.broadcasted_iota(jnp.int32, (tl, tl), 1)
    same = (row // HG_CHUNK) == (col // HG_CHUNK)
    causal = jnp.logical_and(same, row >= col)
    bcum = _ones_dot(causal.astype(BF16), logf)
    btot = _ones_dot(same.astype(BF16), logf)
    q_dec = (hq_ref[0] * jnp.exp(bcum)).astype(BF16)
    k_inv = (kk * jnp.exp(-bcum)).astype(BF16)
    k_end = kk * jnp.exp(btot - bcum)
    e_tot = jnp.exp(btot)
    v = hi_ref[0].astype(BF16)
    rows_tl = lax.broadcasted_iota(jnp.int32, (tl, HG_DK), 0)

    for h in range(HG_HEADS):
        hs = slice(h * HG_DK, (h + 1) * HG_DK)
        att = jnp.where(causal, _dot_nt(q_dec[:, hs], k_inv[:, hs]), 0.0)
        o_scr[:, hs] = _dot(att.astype(BF16), v[:, hs])
        st = st_scr[h]
        for cc in range(nchunk):
            rs = slice(cc * HG_CHUNK, (cc + 1) * HG_CHUNK)
            o_scr[rs, hs] += _dot_nt(q_dec[rs, hs], st.astype(BF16))
            in_chunk = (rows_tl // HG_CHUNK) == cc
            k_c = jnp.where(in_chunk, k_end[:, hs], 0.0).astype(BF16)
            decay = e_tot[cc * HG_CHUNK:cc * HG_CHUNK + 1, hs]
            st = st * decay + _dot_tn(v[:, hs], k_c)
        st_scr[h] = st

    o = o_scr[...]
    gate = _silu(hg_ref[0])
    for h in range(HG_HEADS):
        hs = slice(h * HG_DV, (h + 1) * HG_DV)
        o_ref[0, :, hs] = (_rms(o[:, hs], nw_ref[:, hs]) * gate[:, hs]).astype(o_ref.dtype)

    @pl.when(c == pl.num_programs(1) - 1)
    def _():
        for h in range(HG_HEADS):
            sfin_ref[0, h] = st_scr[h].T


def hgrn_mixer(hq, hf, hi, hg, s0, lb, norm_w, tl, lv, out_dtype):
    bsz, seq, _ = hq.shape
    row3 = lambda b, c: (b, c, 0)
    full2 = lambda b, c: (0, 0)
    st4 = lambda b, c: (b, 0, 0, 0)
    return pl.pallas_call(
        functools.partial(_hgrn_kernel, tl=tl, lv=lv),
        grid=(bsz, seq // tl),
        in_specs=[pl.BlockSpec((1, tl, HG_DIM), row3)] * 4 + [
            pl.BlockSpec((1, HG_HEADS, HG_DK, HG_DV), st4),
            pl.BlockSpec((1, HG_DIM), full2), pl.BlockSpec((1, HG_DIM), full2)],
        out_specs=[pl.BlockSpec((1, tl, HG_DIM), row3), pl.BlockSpec((1, HG_HEADS, HG_DK, HG_DV), st4)],
        out_shape=[jax.ShapeDtypeStruct((bsz, seq, HG_DIM), out_dtype),
                   jax.ShapeDtypeStruct((bsz, HG_HEADS, HG_DK, HG_DV), F32)],
        scratch_shapes=[pltpu.VMEM((HG_HEADS, HG_DV, HG_DK), F32), pltpu.VMEM((tl, HG_DIM), F32)],
        compiler_params=_cparams("parallel", "arbitrary"), name="hgrn_mixer",
    )(hq, hf, hi, hg, s0, lb, norm_w)


def _mla_pre_kernel(cq_ref, ckv_ref, kra_ref, krb_ref, cosq_ref, sinq_ref, cosk_ref, sink_ref, qnw_ref, kvnw_ref,
                    wn_ref, wr_ref, wrs_ref, wukt_ref, ql_ref, qr_ref, c_ref, r_ref):
    n = _rms(cq_ref[...], qnw_ref[...]).astype(BF16)
    q_nope = _dot(n, wn_ref[...]).astype(BF16)
    q_rope = _dot(n, wr_ref[...]) * cosq_ref[...] + _dot(n, wrs_ref[...]) * sinq_ref[...]
    qr_ref[...] = q_rope.astype(qr_ref.dtype)
    for h in range(MLA_HEADS):
        ql_ref[:, h * MLA_KV_LORA:(h + 1) * MLA_KV_LORA] = _dot(
            q_nope[:, h * MLA_NOPE:(h + 1) * MLA_NOPE], wukt_ref[h]).astype(ql_ref.dtype)
    c_ref[...] = _rms(ckv_ref[...], kvnw_ref[...])
    r = kra_ref[...] * cosk_ref[...] + krb_ref[...] * sink_ref[...]
    r_ref[...] = r[:, :MLA_ROPE]


def mla_pre(cq, ckv, kra, krb, tabs, qnw, kvnw, wn, wr, wrs, wukt, tm=512):
    m = cq.shape[0]
    tm = min(tm, m, tabs[0].shape[0])
    nblk = tabs[0].shape[0] // tm
    rowb = lambda i: (i, 0)
    tabb = lambda i: (i % nblk, 0)
    full2 = lambda i: (0, 0)
    hq = MLA_HEADS * MLA_ROPE
    return pl.pallas_call(
        _mla_pre_kernel,
        grid=(m // tm,),
        in_specs=[pl.BlockSpec((tm, MLA_Q_LORA), rowb), pl.BlockSpec((tm, MLA_KV_LORA), rowb),
                  pl.BlockSpec((tm, LANES), rowb), pl.BlockSpec((tm, LANES), rowb),
                  pl.BlockSpec((tm, hq), tabb), pl.BlockSpec((tm, hq), tabb),
                  pl.BlockSpec((tm, LANES), tabb), pl.BlockSpec((tm, LANES), tabb),
                  pl.BlockSpec((1, MLA_Q_LORA), full2), pl.BlockSpec((1, MLA_KV_LORA), full2),
                  pl.BlockSpec(wn.shape, full2), pl.BlockSpec(wr.shape, full2), pl.BlockSpec(wrs.shape, full2),
                  pl.BlockSpec(wukt.shape, lambda i: (0, 0, 0))],
        out_specs=[pl.BlockSpec((tm, MLA_HEADS * MLA_KV_LORA), rowb), pl.BlockSpec((tm, hq), rowb),
                   pl.BlockSpec((tm, MLA_KV_LORA), rowb), pl.BlockSpec((tm, MLA_ROPE), rowb)],
        out_shape=[jax.ShapeDtypeStruct((m, MLA_HEADS * MLA_KV_LORA), BF16), jax.ShapeDtypeStruct((m, hq), BF16),
                   jax.ShapeDtypeStruct((m, MLA_KV_LORA), F32), jax.ShapeDtypeStruct((m, MLA_ROPE), F32)],
        compiler_params=_cparams("parallel"), name="mla_pre",
    )(cq, ckv, kra, krb, *tabs, qnw, kvnw, wn, wr, wrs, wukt)


def _mla_softmax_step(s_list, cblks, m, l, acc):
    def fold(x, op):
        parts = [x[:, b * LANES:(b + 1) * LANES] for b in range(x.shape[1] // LANES)]
        return functools.reduce(op, parts)

    smax = functools.reduce(jnp.maximum, [fold(s, jnp.maximum) for s in s_list])
    m_new = jnp.maximum(m, jnp.max(smax, axis=-1, keepdims=True))
    alpha = jnp.exp(m - m_new)
    l = alpha * l
    acc = jnp.concatenate([alpha] * (acc.shape[1] // LANES), axis=1) * acc
    for s, cblk in zip(s_list, cblks):
        p = jnp.exp(s - jnp.concatenate([m_new] * (s.shape[1] // LANES), axis=1))
        l = l + fold(p, jnp.add)
        acc = acc + _dot(p.astype(BF16), cblk)
    return m_new, l, acc


def _mla_kernel(ql_ref, qr_ref, c_ref, r_ref, wuv_ref, o_ref, qs_scr, qrs_scr, cb_scr, rb_scr, m_scr, l_scr, acc_scr,
                *, tq, tk):
    i = pl.program_id(1)

    @pl.when(i == 0)
    def _():
        cb_scr[...] = c_ref[0].astype(BF16)
        rb_scr[...] = r_ref[0].astype(BF16)

    for h in range(MLA_HEADS):
        qs_scr[h * tq:(h + 1) * tq, :] = ql_ref[0, :, h * MLA_KV_LORA:(h + 1) * MLA_KV_LORA]
        qrs_scr[h * tq:(h + 1) * tq, :] = qr_ref[0, :, h * MLA_ROPE:(h + 1) * MLA_ROPE]
    rows = MLA_HEADS * tq

    def step(j, m, l, acc, valid):
        start = pl.multiple_of(j * tk, tk)
        cblk = cb_scr[pl.ds(start, tk), :]
        s = (_dot_nt(qs_scr[...], cblk) + _dot_nt(qrs_scr[...], rb_scr[pl.ds(start, tk), :])) * MLA_SCALE
        if valid is not None:
            s = jnp.where(valid, s, NEG_BIG)
        return _mla_softmax_step([s], [cblk], m, l, acc)

    nfull = (i * tq) // tk
    q_idx = i * tq + lax.broadcasted_iota(jnp.int32, (rows, tk), 0) % tq
    k_idx = nfull * tk + lax.broadcasted_iota(jnp.int32, (rows, tk), 1)
    m, l, acc = step(nfull, jnp.full((rows, LANES), NEG_BIG, F32), jnp.zeros((rows, LANES), F32),
                     jnp.zeros((rows, MLA_KV_LORA), F32), k_idx <= q_idx)
    m_scr[...] = m
    l_scr[...] = l
    acc_scr[...] = acc

    def body(j, carry):
        m, l, acc = step(j, m_scr[...], l_scr[...], acc_scr[...], None)
        m_scr[...] = m
        l_scr[...] = l
        acc_scr[...] = acc
        return carry

    lax.fori_loop(0, nfull, body, 0)
    ctx = (acc_scr[...] / jnp.sum(l_scr[...], axis=-1, keepdims=True)).astype(BF16)
    for h in range(MLA_HEADS):
        o_ref[0, :, h * MLA_V:(h + 1) * MLA_V] = _dot(ctx[h * tq:(h + 1) * tq], wuv_ref[h]).astype(o_ref.dtype)


def mla_attention(ql, qr, c, r, wuv, tq=128, tk=512):
    bsz, seq, _ = ql.shape
    tk = min(tk, seq)
    rows = MLA_HEADS * tq
    return pl.pallas_call(
        functools.partial(_mla_kernel, tq=tq, tk=tk),
        grid=(bsz, seq // tq),
        in_specs=[pl.BlockSpec((1, tq, MLA_HEADS * MLA_KV_LORA), lambda b, i: (b, i, 0)),
                  pl.BlockSpec((1, tq, MLA_HEADS * MLA_ROPE), lambda b, i: (b, i, 0)),
                  pl.BlockSpec((1, seq, MLA_KV_LORA), lambda b, i: (b, 0, 0)),
                  pl.BlockSpec((1, seq, MLA_ROPE), lambda b, i: (b, 0, 0)),
                  pl.BlockSpec(wuv.shape, lambda b, i: (0, 0, 0))],
        out_specs=pl.BlockSpec((1, tq, MLA_HEADS * MLA_V), lambda b, i: (b, i, 0)),
        out_shape=jax.ShapeDtypeStruct((bsz, seq, MLA_HEADS * MLA_V), BF16),
        scratch_shapes=[pltpu.VMEM((rows, MLA_KV_LORA), BF16), pltpu.VMEM((rows, MLA_ROPE), BF16),
                        pltpu.VMEM((seq, MLA_KV_LORA), BF16), pltpu.VMEM((seq, MLA_ROPE), BF16),
                        pltpu.VMEM((rows, LANES), F32), pltpu.VMEM((rows, LANES), F32),
                        pltpu.VMEM((rows, MLA_KV_LORA), F32)],
        compiler_params=_cparams("parallel", "arbitrary"), name="mla_attention")(ql, qr, c, r, wuv)


def _mla_pages(qs, qrs, cs, rts, m, l, acc, valid):
    s_list = [(_dot_nt(qs, c) + _dot(qrs, rt)) * MLA_SCALE for c, rt in zip(cs, rts)]
    if valid is not None:
        s_list = [jnp.where(valid, s, NEG_BIG) for s in s_list]
    return _mla_softmax_step(s_list, cs, m, l, acc)


def _mla_dec_kernel(pt_ref, ql_ref, qr_ref, cn_ref, rn_ref, *rest, pp, lq):
    del pt_ref
    c_refs, r_refs = rest[:pp], rest[pp:2 * pp]
    o_ref, m_scr, l_scr, acc_scr = rest[2 * pp:]
    s = pl.program_id(1)
    qs = ql_ref[0]
    qrs = qr_ref[0]
    rows = qs.shape[0]

    @pl.when(s == 0)
    def _():
        r2 = lax.broadcasted_iota(jnp.int32, (rows, PAGE_SIZE), 0)
        c2 = lax.broadcasted_iota(jnp.int32, (rows, PAGE_SIZE), 1)
        m, l, acc = _mla_pages(qs, qrs, [cn_ref[0].astype(BF16)], [rn_ref[0].astype(BF16)],
                               jnp.full((rows, LANES), NEG_BIG, F32), jnp.zeros((rows, LANES), F32),
                               jnp.zeros((rows, MLA_KV_LORA), F32), c2 <= r2 % lq)
        m_scr[...] = m
        l_scr[...] = l
        acc_scr[...] = acc

    m, l, acc = _mla_pages(qs, qrs, [r[0].astype(BF16) for r in c_refs], [r[0].astype(BF16) for r in r_refs],
                           m_scr[...], l_scr[...], acc_scr[...], None)
    m_scr[...] = m
    l_scr[...] = l
    acc_scr[...] = acc

    @pl.when(s == pl.num_programs(1) - 1)
    def _():
        o_ref[0] = acc / jnp.sum(l, axis=-1, keepdims=True)


def mla_decode(ql, qr, c_new, rt_new, c_pool, rt_pool, page_table, lq):
    bsz, rows, _ = ql.shape
    n_pages = page_table.shape[1]
    pp = _pages_per_step(n_pages)

    def page_map(u):
        return lambda b, s, pt: (pt[b, s * pp + u], 0, 0)

    per_b = lambda b, s, pt: (b, 0, 0)
    grid_spec = pltpu.PrefetchScalarGridSpec(
        num_scalar_prefetch=1, grid=(bsz, n_pages // pp),
        in_specs=[pl.BlockSpec((1, rows, MLA_KV_LORA), per_b), pl.BlockSpec((1, rows, MLA_ROPE), per_b),
                  pl.BlockSpec((1, PAGE_SIZE, MLA_KV_LORA), per_b), pl.BlockSpec((1, MLA_ROPE, PAGE_SIZE), per_b)]
        + [pl.BlockSpec((1, PAGE_SIZE, MLA_KV_LORA), page_map(u)) for u in range(pp)]
        + [pl.BlockSpec((1, MLA_ROPE, PAGE_SIZE), page_map(u)) for u in range(pp)],
        out_specs=pl.BlockSpec((1, rows, MLA_KV_LORA), per_b),
        scratch_shapes=[pltpu.VMEM((rows, LANES), F32), pltpu.VMEM((rows, LANES), F32),
                        pltpu.VMEM((rows, MLA_KV_LORA), F32)])
    return pl.pallas_call(
        functools.partial(_mla_dec_kernel, pp=pp, lq=lq), grid_spec=grid_spec,
        out_shape=jax.ShapeDtypeStruct((bsz, rows, MLA_KV_LORA), F32),
        compiler_params=_cparams("parallel", "arbitrary"), name="mla_decode",
    )(page_table, ql, qr, c_new, rt_new, *([c_pool] * pp), *([rt_pool] * pp))


def _head_linear_kernel(x_ref, w_ref, o_ref):
    o_ref[0] = _dot(x_ref[0].astype(BF16), w_ref[0])


def head_linear(x, w):
    nh, m, k = x.shape
    n = w.shape[2]
    return pl.pallas_call(
        _head_linear_kernel, grid=(nh,),
        in_specs=[pl.BlockSpec((1, m, k), lambda h: (h, 0, 0)), pl.BlockSpec((1, k, n), lambda h: (h, 0, 0))],
        out_specs=pl.BlockSpec((1, m, n), lambda h: (h, 0, 0)),
        out_shape=jax.ShapeDtypeStruct((nh, m, n), F32),
        compiler_params=_cparams("parallel"), name="head_linear")(x, w)


def _pad_cols(w, n):
    return jnp.pad(w, ((0, 0), (0, n - w.shape[1])))


def _lane_row(v):
    return jnp.pad(v.astype(F32), (0, LANES - v.shape[0]))[None, :]


def _rope_tables(pos):
    half = MLA_ROPE // 2
    inv = ROPE_THETA ** (-jnp.arange(half, dtype=F32) / half)
    ang = pos[:, None] * inv[None, :]
    cos = jnp.concatenate([jnp.cos(ang), jnp.cos(ang)], axis=-1)
    sin = jnp.concatenate([-jnp.sin(ang), jnp.sin(ang)], axis=-1)
    cosq, sinq = jnp.tile(cos, (1, MLA_HEADS)), jnp.tile(sin, (1, MLA_HEADS))
    cosk, sink = _pad_cols(cos, LANES), _pad_cols(sin, LANES)
    return cosq, sinq, cosk, sink


def _swap_halves(w):
    half = MLA_ROPE // 2
    g = w.reshape(w.shape[0], -1, 2, half)
    return g[:, :, ::-1, :].reshape(w.shape)


def _even_weights(w_in, w_out):
    cuts = np.cumsum([SSM_INNER, SSM_CONV_DIM, SSM_HEADS, SB_Q_DIM, SB_KV_DIM, SB_KV_DIM])[:-1].tolist()
    wz, wxbc, wdt, wq, wk, wv = jnp.split(w_in.astype(BF16), cuts, axis=1)
    wo = w_out.astype(BF16)
    return (wz, wxbc, _pad_cols(wdt, LANES), wq, wk, wv), (wo[:SSM_INNER], wo[SSM_INNER:])


def _odd_weights(w_in, w_out):
    cuts = np.cumsum([HG_DIM, HG_DIM, HG_DIM, HG_DIM, MLA_Q_LORA, MLA_KV_LORA, MLA_ROPE])[:-1].tolist()
    whq, whf, whi, whg, wcq, wckv, wkr = jnp.split(w_in.astype(BF16), cuts, axis=1)
    wo = w_out.astype(BF16)
    ws = (whq, whf, whi, whg, wcq, wckv, _pad_cols(wkr, LANES), _pad_cols(_swap_halves(wkr), LANES))
    return ws, (wo[:HG_DIM], wo[HG_DIM:])


def _pad_rows(a, n):
    return jnp.pad(a, ((0, 0), (0, n - a.shape[1]), (0, 0)))


def _transposed_pages(pool):
    npool = pool.shape[0]
    return jnp.transpose(pool, (0, 2, 3, 1)).reshape(npool, -1, PAGE_SIZE)


def _even_layer(x, sample, page_table, w_in, w_out, conv_w, conv_b, dt_bias, a_log, d_skip, norm_w,
                conv0, s0, k_pool, v_pool):
    bsz, seq, d = x.shape
    ws, (wo_a, wo_b) = _even_weights(w_in, w_out)
    z, xbc, dt, q, k, v = multi_linear(x.reshape(bsz * seq, d), ws, (F32, F32, F32, BF16, F32, F32))
    xbc3, z3, dt3 = (a.reshape(bsz, seq, -1) for a in (xbc, z, dt))
    conv_new = jnp.concatenate([conv0, xbc3], axis=1)[:, -(SSM_CONV - 1):]
    lv = seq
    if seq < SSM_CHUNK:
        lv = seq
        xbc3, z3, dt3 = (_pad_rows(a, SSM_CHUNK) for a in (xbc3, z3, dt3))
    y_a, s_new = ssd_mixer(xbc3, z3, dt3, conv0, s0, conv_w, conv_b[None, :], _lane_row(dt_bias), _lane_row(a_log),
                           jnp.repeat(d_skip.astype(F32), SSM_HEAD_DIM)[None, :], norm_w[None, :],
                           min(lv, SSM_CHUNK), F32 if sample else BF16)
    y_a = y_a[:, :seq].reshape(bsz * seq, SSM_INNER)
    q3 = q.reshape(bsz, seq, SB_Q_DIM)
    k3, v3 = k.reshape(bsz, seq, SB_KV_DIM), v.reshape(bsz, seq, SB_KV_DIM)
    if sample:
        group = SB_HEADS // SB_KV_HEADS
        q5 = (q3 * jnp.asarray(SB_HEAD_DIM ** -0.5, BF16)).reshape(bsz, seq, SB_KV_HEADS, group, SB_HEAD_DIM)
        q5 = jnp.transpose(q5, (0, 2, 3, 1, 4)).reshape(bsz, SB_KV_HEADS, group * seq, SB_HEAD_DIM)
        q_bd = jnp.einsum('bkxd,kj->bkxjd', q5, jnp.eye(SB_KV_HEADS, dtype=BF16))
        q_bd = q_bd.reshape(bsz, SB_KV_HEADS * group * seq, SB_KV_DIM)
        tr = lambda a: jnp.transpose(_pad_rows(a, PAGE_SIZE), (0, 2, 1))
        acc = sb_decode(q_bd, tr(k3), tr(v3), _transposed_pages(k_pool), _transposed_pages(v_pool), page_table, seq)
        acc = acc.reshape(bsz, SB_KV_HEADS, group, seq, SB_KV_HEADS, SB_HEAD_DIM)
        y_b = jnp.einsum('bkrtkd->btkrd', acc).reshape(bsz * seq, SB_Q_DIM)
    else:
        y_b = sb_attention(q3, k3, v3).reshape(bsz * seq, SB_Q_DIM)
    return (y_a, y_b), (wo_a, wo_b), s_new, conv_new, k3.reshape(bsz, seq, SB_KV_HEADS, SB_HEAD_DIM), \
        v3.reshape(bsz, seq, SB_KV_HEADS, SB_HEAD_DIM)


def _odd_layer(x, sample, pos_start, page_table, lb, w_in, w_out, hg_norm_w, q_norm_w, kv_norm_w, w_uq, w_uk, w_uv,
               s0, c_pool, r_pool):
    bsz, seq, d = x.shape
    m = bsz * seq
    ws, (wo_c, wo_d) = _odd_weights(w_in, w_out)
    hq, hf, hi, hg, cq, ckv, kra, krb = multi_linear(x.reshape(m, d), ws, (F32,) * 8)
    h3 = [a.reshape(bsz, seq, HG_DIM) for a in (hq, hf, hi, hg)]
    tl = min(256, seq)
    lv = tl
    if seq < HG_CHUNK:
        tl, lv = HG_CHUNK, seq
        h3 = [_pad_rows(a, HG_CHUNK) for a in h3]
    o_c, s_new = hgrn_mixer(*h3, s0, lb[None, :].astype(F32), hg_norm_w.reshape(1, HG_DIM), tl, lv,
                            F32 if sample else BF16)
    o_c = o_c[:, :seq].reshape(m, HG_DIM)
    pos = jnp.arange(seq, dtype=F32) + pos_start
    tabs = _rope_tables(pos)
    if sample:
        tabs = tuple(jnp.tile(t, (bsz, 1)) for t in tabs)
    w_uq_b = w_uq.astype(BF16)
    wn = w_uq_b[:, :, :MLA_NOPE].reshape(MLA_Q_LORA, MLA_HEADS * MLA_NOPE)
    wr = w_uq_b[:, :, MLA_NOPE:].reshape(MLA_Q_LORA, MLA_HEADS * MLA_ROPE)
    wukt = jnp.transpose(w_uk.astype(BF16), (1, 2, 0))
    wuv = jnp.transpose(w_uv.astype(BF16), (1, 0, 2))
    ql, qr, c_new, r_new = mla_pre(cq, ckv, kra, krb, tabs, q_norm_w[None, :], kv_norm_w[None, :],
                                   wn, wr, _swap_halves(wr), wukt)
    c3, r3 = c_new.reshape(bsz, seq, MLA_KV_LORA), r_new.reshape(bsz, seq, MLA_ROPE)
    if sample:
        ql3 = jnp.transpose(ql.reshape(bsz, seq, MLA_HEADS, MLA_KV_LORA), (0, 2, 1, 3))
        qr3 = jnp.transpose(qr.reshape(bsz, seq, MLA_HEADS, MLA_ROPE), (0, 2, 1, 3))
        ctx = mla_decode(ql3.reshape(bsz, MLA_HEADS * seq, MLA_KV_LORA), qr3.reshape(bsz, MLA_HEADS * seq, MLA_ROPE),
                         _pad_rows(c3, PAGE_SIZE), jnp.transpose(_pad_rows(r3, PAGE_SIZE), (0, 2, 1)), c_pool,
                         jnp.transpose(r_pool, (0, 2, 1)), page_table, seq)
        ctx = jnp.transpose(ctx.reshape(bsz, MLA_HEADS, seq, MLA_KV_LORA), (1, 0, 2, 3))
        o_d = head_linear(ctx.reshape(MLA_HEADS, m, MLA_KV_LORA), wuv)
        o_d = jnp.transpose(o_d, (1, 0, 2)).reshape(m, MLA_HEADS * MLA_V)
    else:
        o_d = mla_attention(ql.reshape(bsz, seq, -1), qr.reshape(bsz, seq, -1), c3, r3, wuv)
        o_d = o_d.reshape(m, MLA_HEADS * MLA_V)
    return (o_c, o_d), (wo_c, wo_d), s_new, c3, r3


def _hgrn_lower_bound(gamma, layer):
    p = jax.nn.softmax(gamma.astype(F32), axis=0)
    return jnp.cumsum(p, axis=0)[layer] - p[0]


def kernel(x_prompt, x_sample, mem_prompt, state_ssm, state_conv, cache_sb_k, cache_sb_v, state_hgrn,
           cache_mla_latent, cache_mla_rope, cache_mem_k, cache_mem_v, page_table,
           w_in_even, w_out_even, ssm_conv_w, ssm_conv_b, ssm_dt_bias, ssm_a_log, ssm_d, ssm_norm_w,
           w_in_odd, w_out_odd, hg_lower_bound, hg_norm_w, mla_q_norm_w, mla_kv_norm_w, mla_w_uq, mla_w_uk, mla_w_uv,
           w_mem_q, w_mem_k, w_mem_v, w_mem_o, ffn_w_gate, ffn_w_up, ffn_w_down, ln_g, ln_b):
    past_len = page_table.shape[1] * PAGE_SIZE
    mem_len = mem_prompt.shape[1]

    def run(x, sample):
        bsz, seq, d = x.shape
        m = bsz * seq
        pos_start = past_len if sample else 0
        outs = dict(ssm=[], conv=[], sbk=[], sbv=[], hg=[], lat=[], rope=[], mk=[], mv=[])
        x2 = x.reshape(m, d)
        for l in range(DEPTH):
            i = l // 2
            if l % 2 == 0:
                if sample:
                    conv0, s0 = state_conv[i], state_ssm[i]
                else:
                    conv0 = jnp.zeros((bsz, SSM_CONV - 1, SSM_CONV_DIM), F32)
                    s0 = jnp.zeros((bsz, SSM_HEADS, SSM_HEAD_DIM, SSM_STATE), F32)
                ys, wos, s1, conv1, k_new, v_new = _even_layer(
                    x2.reshape(bsz, seq, d), sample, page_table, w_in_even[i], w_out_even[i], ssm_conv_w[i],
                    ssm_conv_b[i], ssm_dt_bias[i], ssm_a_log[i], ssm_d[i], ssm_norm_w[i], conv0, s0,
                    cache_sb_k[i], cache_sb_v[i])
                outs['ssm'].append(s1)
                outs['conv'].append(conv1)
                outs['sbk'].append(k_new)
                outs['sbv'].append(v_new)
            else:
                s0 = state_hgrn[i] if sample else jnp.zeros((bsz, HG_HEADS, HG_DK, HG_DV), F32)
                ys, wos, s1, c_new, r_new = _odd_layer(
                    x2.reshape(bsz, seq, d), sample, pos_start, page_table, _hgrn_lower_bound(hg_lower_bound, l),
                    w_in_odd[i], w_out_odd[i], hg_norm_w[i], mla_q_norm_w[i], mla_kv_norm_w[i], mla_w_uq[i],
                    mla_w_uk[i], mla_w_uv[i], s0, cache_mla_latent[i], cache_mla_rope[i])
                outs['hg'].append(s1)
                outs['lat'].append(c_new)
                outs['rope'].append(r_new)
            x2 = proj_ln(list(ys), list(wos), x2, ln_g[l, 0][None, :], ln_b[l, 0][None, :])
            if sample:
                mk = cache_mem_k[l].reshape(bsz, mem_len, MEM_INNER)
                mv = cache_mem_v[l].reshape(bsz, mem_len, MEM_INNER)
                bb, tl = 8, seq
            else:
                mk, mv = multi_linear(mem_prompt.reshape(bsz * mem_len, d),
                                      (w_mem_k[l].astype(BF16), w_mem_v[l].astype(BF16)), (F32, F32))
                mk, mv = mk.reshape(bsz, mem_len, MEM_INNER), mv.reshape(bsz, mem_len, MEM_INNER)
                outs['mk'].append(mk.reshape(bsz, mem_len, MEM_HEADS, MEM_HEAD_DIM))
                outs['mv'].append(mv.reshape(bsz, mem_len, MEM_HEADS, MEM_HEAD_DIM))
                bb, tl = 1, min(1024, seq)
            (q,) = multi_linear(x2, (w_mem_q[l].astype(BF16),), (BF16,))
            o = cross_attend(q.reshape(bsz, seq, MEM_INNER), mk, mv, bb, tl).reshape(m, MEM_INNER)
            x2 = proj_ln([o], [w_mem_o[l].astype(BF16)], x2, ln_g[l, 1][None, :], ln_b[l, 1][None, :])
            x2 = ffn_ln(x2, ffn_w_gate[l].astype(BF16), ffn_w_up[l].astype(BF16), ffn_w_down[l].astype(BF16),
                        ln_g[l, 2][None, :], ln_b[l, 2][None, :])
        stk = lambda a: jnp.stack(a, axis=0)
        return x2.reshape(bsz, seq, d), {k_: (stk(v_) if v_ else None) for k_, v_ in outs.items()}

    y_prompt, p = run(x_prompt, False)
    y_sample, s = run(x_sample, True)
    return (y_prompt, y_sample, p['ssm'], p['conv'], p['sbk'], p['sbv'], p['hg'], p['lat'], p['rope'], p['mk'],
            p['mv'], s['ssm'], s['conv'], s['sbk'], s['sbv'], s['hg'], s['lat'], s['rope'])
```

```python
import functools

import numpy as np
import jax
import jax.numpy as jnp
from jax import lax
from jax.experimental import pallas as pl
from jax.experimental.pallas import tpu as pltpu

F32 = jnp.float32
BF16 = jnp.bfloat16

D_MODEL = 1024
DEPTH = 2
PAGE_SIZE = 128

SSM_HEAD_DIM = 64
SSM_HEADS = 16
SSM_INNER = SSM_HEADS * SSM_HEAD_DIM
SSM_GROUPS = 2
SSM_STATE = 128
SSM_CONV = 4
SSM_CONV_DIM = SSM_INNER + 2 * SSM_GROUPS * SSM_STATE
SSM_CHUNK = 128

SB_HEADS = 8
SB_KV_HEADS = 4
SB_HEAD_DIM = 64
SB_Q_DIM = SB_HEADS * SB_HEAD_DIM
SB_KV_DIM = SB_KV_HEADS * SB_HEAD_DIM

HG_HEADS = 4
HG_DK = 128
HG_DV = 128
HG_CHUNK = 32
HG_DIM = HG_HEADS * HG_DK

MLA_HEADS = 8
MLA_Q_LORA = 384
MLA_KV_LORA = 256
MLA_NOPE = 64
MLA_ROPE = 32
MLA_V = 64
MLA_SCALE = (MLA_NOPE + MLA_ROPE) ** -0.5
ROPE_THETA = 10000.0

MEM_HEADS = 4
MEM_HEAD_DIM = 128
MEM_INNER = MEM_HEADS * MEM_HEAD_DIM

DEEPNORM_ALPHA = (2 * DEPTH) ** 0.25

LANES = 128
VMEM_LIMIT_BYTES = 48 * 1024 * 1024
NEG_BIG = -1e30
SB_DEAD_LOG = -120.0
MLA_ROW_GROUPS = 2
DECODE_PAGES_PER_STEP = 32
DECODE_PAGE_GROUPS = 1


def _cparams(*sem):
    return pltpu.CompilerParams(dimension_semantics=sem, vmem_limit_bytes=VMEM_LIMIT_BYTES)


def _sigmoid(x):
    return 0.5 * jnp.tanh(0.5 * x) + 0.5


def _silu(x):
    return x * _sigmoid(x)


def _softplus(x):
    return jnp.maximum(x, 0.0) + jnp.log1p(jnp.exp(-jnp.abs(x)))


def _dot(a, b):
    return jnp.dot(a, b, preferred_element_type=F32)


def _dot_nt(a, b):
    return lax.dot_general(a, b, (((1,), (1,)), ((), ())), preferred_element_type=F32)


def _dot_tn(a, b):
    return lax.dot_general(a, b, (((0,), (0,)), ((), ())), preferred_element_type=F32)


def _split3(x):
    hi = x.astype(BF16)
    r1 = x - hi.astype(F32)
    mid = r1.astype(BF16)
    lo = (r1 - mid.astype(F32)).astype(BF16)
    return hi, mid, lo


def _ones_dot(mask01, x):
    hi, mid, lo = _split3(x)
    return _dot(mask01, hi) + _dot(mask01, mid) + _dot(mask01, lo)


def _dot_ones(x, mask01):
    hi, mid, lo = _split3(x)
    return _dot(hi, mask01) + _dot(mid, mask01) + _dot(lo, mask01)


def _layer_norm(x, g, b):
    xc = x - jnp.mean(x, axis=-1, keepdims=True)
    var = jnp.mean(xc * xc, axis=-1, keepdims=True)
    return xc * lax.rsqrt(var + 1e-5) * g + b


def _rms(x, w, eps=1e-6):
    return x * lax.rsqrt(jnp.mean(x * x, axis=-1, keepdims=True) + eps) * w


def _multi_linear_kernel(x_ref, *refs, n):
    x = x_ref[...].astype(BF16)
    for w_ref, o_ref in zip(refs[:n], refs[n:]):
        o_ref[...] = _dot(x, w_ref[...]).astype(o_ref.dtype)


def multi_linear(x, ws, out_dtypes, tm=512):
    m, k = x.shape
    tm = min(tm, m)
    n = len(ws)
    in_specs = [pl.BlockSpec((tm, k), lambda i: (i, 0))]
    in_specs += [pl.BlockSpec(w.shape, lambda i: (0, 0)) for w in ws]
    out_specs = [pl.BlockSpec((tm, w.shape[1]), lambda i: (i, 0)) for w in ws]
    out_shape = [jax.ShapeDtypeStruct((m, w.shape[1]), dt) for w, dt in zip(ws, out_dtypes)]
    return pl.pallas_call(
        functools.partial(_multi_linear_kernel, n=n),
        grid=(m // tm,), in_specs=in_specs, out_specs=out_specs, out_shape=out_shape,
        compiler_params=_cparams("parallel"), name="multi_linear")(x, *ws)


def _proj_ln_kernel(*refs, n):
    a_refs, w_refs = refs[:n], refs[n:2 * n]
    r_ref, g_ref, b_ref, o_ref = refs[2 * n:]
    y = _dot(a_refs[0][...].astype(BF16), w_refs[0][...])
    for a_ref, w_ref in zip(a_refs[1:], w_refs[1:]):
        y = y + _dot(a_ref[...].astype(BF16), w_ref[...])
    o_ref[...] = _layer_norm(DEEPNORM_ALPHA * r_ref[...] + y, g_ref[...], b_ref[...])


def proj_ln(a_list, w_list, resid, g, b, tm=512):
    m, d = resid.shape
    tm = min(tm, m)
    n = len(a_list)
    in_specs = [pl.BlockSpec((tm, a.shape[1]), lambda i: (i, 0)) for a in a_list]
    in_specs += [pl.BlockSpec(w.shape, lambda i: (0, 0)) for w in w_list]
    in_specs += [pl.BlockSpec((tm, d), lambda i: (i, 0)),
                 pl.BlockSpec((1, d), lambda i: (0, 0)), pl.BlockSpec((1, d), lambda i: (0, 0))]
    return pl.pallas_call(
        functools.partial(_proj_ln_kernel, n=n),
        grid=(m // tm,), in_specs=in_specs, out_specs=pl.BlockSpec((tm, d), lambda i: (i, 0)),
        out_shape=jax.ShapeDtypeStruct((m, d), F32),
        compiler_params=_cparams("parallel"), name="proj_ln")(*a_list, *w_list, resid, g, b)


def _ffn_kernel(x_ref, wg_ref, wu_ref, wd_ref, g_ref, b_ref, o_ref, acc_ref):
    j = pl.program_id(1)
    x = x_ref[...]
    xb = x.astype(BF16)
    gate = _dot(xb, wg_ref[...])
    up = _dot(xb, wu_ref[...])
    part = _dot((_silu(gate) * up).astype(BF16), wd_ref[...])

    @pl.when(j == 0)
    def _():
        acc_ref[...] = part

    @pl.when(j > 0)
    def _():
        acc_ref[...] += part

    @pl.when(j == pl.num_programs(1) - 1)
    def _():
        o_ref[...] = _layer_norm(DEEPNORM_ALPHA * x + acc_ref[...], g_ref[...], b_ref[...])


def ffn_ln(x, wg, wu, wd, g, b, tm=512, th=1408):
    m, d = x.shape
    hdim = wg.shape[1]
    tm = min(tm, m)
    return pl.pallas_call(
        _ffn_kernel,
        grid=(m // tm, hdim // th),
        in_specs=[pl.BlockSpec((tm, d), lambda i, j: (i, 0)),
                  pl.BlockSpec((d, th), lambda i, j: (0, j)),
                  pl.BlockSpec((d, th), lambda i, j: (0, j)),
                  pl.BlockSpec((th, d), lambda i, j: (j, 0)),
                  pl.BlockSpec((1, d), lambda i, j: (0, 0)),
                  pl.BlockSpec((1, d), lambda i, j: (0, 0))],
        out_specs=pl.BlockSpec((tm, d), lambda i, j: (i, 0)),
        out_shape=jax.ShapeDtypeStruct((m, d), F32),
        scratch_shapes=[pltpu.VMEM((tm, d), F32)],
        compiler_params=_cparams("parallel", "arbitrary"), name="ffn_ln")(x, wg, wu, wd, g, b)


def _cross_kernel(q_ref, k_ref, v_ref, o_ref):
    scale = MEM_HEAD_DIM ** -0.5
    for h in range(MEM_HEADS):
        sl = slice(h * MEM_HEAD_DIM, (h + 1) * MEM_HEAD_DIM)
        qh = q_ref[:, :, sl]
        kh = k_ref[:, :, sl].astype(BF16)
        vh = v_ref[:, :, sl].astype(BF16)
        s = jnp.einsum('bqd,bkd->bqk', qh, kh, preferred_element_type=F32) * scale
        e = jnp.exp(s - jnp.max(s, axis=-1, keepdims=True))
        p = e / jnp.sum(e, axis=-1, keepdims=True)
        oh = jnp.einsum('bqk,bkd->bqd', p.astype(BF16), vh, preferred_element_type=F32)
        o_ref[:, :, sl] = oh.astype(o_ref.dtype)


def cross_attend(q, mk, mv, bb, tl):
    bsz, seq, d = q.shape
    mlen = mk.shape[1]
    return pl.pallas_call(
        _cross_kernel,
        grid=(bsz // bb, seq // tl),
        in_specs=[pl.BlockSpec((bb, tl, d), lambda i, j: (i, j, 0)),
                  pl.BlockSpec((bb, mlen, d), lambda i, j: (i, 0, 0)),
                  pl.BlockSpec((bb, mlen, d), lambda i, j: (i, 0, 0))],
        out_specs=pl.BlockSpec((bb, tl, d), lambda i, j: (i, j, 0)),
        out_shape=jax.ShapeDtypeStruct((bsz, seq, d), BF16),
        compiler_params=_cparams("parallel", "arbitrary"), name="cross_attend")(q, mk, mv)


def _ssd_kernel(xbc_ref, z_ref, dt_ref, conv0_ref, s0_ref, cw_ref, cb_ref, dtb_ref, alog_ref, dsk_ref, nw_ref,
                y_ref, sfin_ref, state_scr, xpad_scr, y_scr, *, q, lv):
    c = pl.program_id(1)
    pad = 8

    @pl.when(c == 0)
    def _():
        state_scr[...] = s0_ref[0]
        xpad_scr[0:pad, :] = jnp.zeros((pad, SSM_CONV_DIM), F32)
        xpad_scr[pad - (SSM_CONV - 1):pad, :] = conv0_ref[0]

    xpad_scr[pad:pad + q, :] = xbc_ref[0]
    conv = cb_ref[...]
    for tap in range(SSM_CONV):
        back = SSM_CONV - 1 - tap
        conv = conv + cw_ref[tap:tap + 1, :] * xpad_scr[pad - back:pad - back + q, :]
    xpad_scr[pad - (SSM_CONV - 1):pad, :] = xpad_scr[pad + q - (SSM_CONV - 1):pad + q, :]

    u = _silu(conv)
    xs = u[:, :SSM_INNER]
    bm = u[:, SSM_INNER:SSM_INNER + SSM_GROUPS * SSM_STATE].astype(BF16)
    cm = u[:, SSM_INNER + SSM_GROUPS * SSM_STATE:].astype(BF16)

    row = lax.broadcasted_iota(jnp.int32, (q, q), 0)
    col = lax.broadcasted_iota(jnp.int32, (q, q), 1)
    tri = row >= col
    dt = _softplus(dt_ref[0] + dtb_ref[...])
    if lv < q:
        dt = jnp.where(lax.broadcasted_iota(jnp.int32, dt.shape, 0) < lv, dt, 0.0)
    d_a = dt * (-jnp.exp(alog_ref[...]))
    acum = _ones_dot(tri.astype(BF16), d_a)
    acum_t = acum.T
    dt_t = dt.T
    a_last = acum_t[:, q - 1:q]
    dend_t = jnp.exp(a_last - acum_t) * dt_t
    chunk_decay = jnp.exp(a_last)
    e_acum = jnp.exp(acum)
    xs_t = xs.T

    heads_per_group = SSM_HEADS // SSM_GROUPS
    for g in range(SSM_GROUPS):
        bm_g = bm[:, g * SSM_STATE:(g + 1) * SSM_STATE]
        cm_g = cm[:, g * SSM_STATE:(g + 1) * SSM_STATE]
        cb = _dot_nt(cm_g, bm_g)
        for r in range(heads_per_group):
            h = g * heads_per_group + r
            hs = slice(h * SSM_HEAD_DIM, (h + 1) * SSM_HEAD_DIM)
            diff = acum[:, h:h + 1] - acum_t[h:h + 1, :]
            wts = cb * jnp.exp(jnp.where(tri, diff, NEG_BIG)) * dt_t[h:h + 1, :]
            s_prev = state_scr[h]
            y_h = _dot(wts.astype(BF16), xs[:, hs].astype(BF16))
            y_h = y_h + _dot_nt(cm_g, s_prev.astype(BF16)) * e_acum[:, h:h + 1]
            y_scr[:, hs] = y_h
            xw = (xs_t[hs, :] * dend_t[h:h + 1, :]).astype(BF16)
            state_scr[h] = s_prev * chunk_decay[h:h + 1, :] + _dot(xw, bm_g)

    y = y_scr[...] + dsk_ref[...] * xs
    y_ref[0] = _rms(y * _silu(z_ref[0]), nw_ref[...]).astype(y_ref.dtype)

    @pl.when(c == pl.num_programs(1) - 1)
    def _():
        sfin_ref[0] = state_scr[...]


def ssd_mixer(xbc, z, dt, conv0, s0, conv_w, conv_b, dt_bias, a_log, d_rep, norm_w, lv, out_dtype):
    bsz, seq, _ = xbc.shape
    q = SSM_CHUNK
    nc = seq // q
    row3 = lambda b, c: (b, c, 0)
    full2 = lambda b, c: (0, 0)
    return pl.pallas_call(
        functools.partial(_ssd_kernel, q=q, lv=lv),
        grid=(bsz, nc),
        in_specs=[pl.BlockSpec((1, q, SSM_CONV_DIM), row3),
                  pl.BlockSpec((1, q, SSM_INNER), row3),
                  pl.BlockSpec((1, q, LANES), row3),
                  pl.BlockSpec((1, SSM_CONV - 1, SSM_CONV_DIM), lambda b, c: (b, 0, 0)),
                  pl.BlockSpec((1, SSM_HEADS, SSM_HEAD_DIM, SSM_STATE), lambda b, c: (b, 0, 0, 0)),
                  pl.BlockSpec((SSM_CONV, SSM_CONV_DIM), full2),
                  pl.BlockSpec((1, SSM_CONV_DIM), full2),
                  pl.BlockSpec((1, LANES), full2),
                  pl.BlockSpec((1, LANES), full2),
                  pl.BlockSpec((1, SSM_INNER), full2),
                  pl.BlockSpec((1, SSM_INNER), full2)],
        out_specs=[pl.BlockSpec((1, q, SSM_INNER), row3),
                   pl.BlockSpec((1, SSM_HEADS, SSM_HEAD_DIM, SSM_STATE), lambda b, c: (b, 0, 0, 0))],
        out_shape=[jax.ShapeDtypeStruct((bsz, seq, SSM_INNER), out_dtype),
                   jax.ShapeDtypeStruct((bsz, SSM_HEADS, SSM_HEAD_DIM, SSM_STATE), F32)],
        scratch_shapes=[pltpu.VMEM((SSM_HEADS, SSM_HEAD_DIM, SSM_STATE), F32),
                        pltpu.VMEM((q + 8, SSM_CONV_DIM), F32),
                        pltpu.VMEM((q, SSM_INNER), F32)],
        compiler_params=_cparams("parallel", "arbitrary"), name="ssd_mixer",
    )(xbc, z, dt, conv0, s0, conv_w, conv_b, dt_bias, a_log, d_rep, norm_w)


def _later_mask(n):
    r = lax.broadcasted_iota(jnp.int32, (n, n), 0)
    c = lax.broadcasted_iota(jnp.int32, (n, n), 1)
    return (r > c).astype(BF16)


def _sb_logs(s, valid):
    t = jnp.log1p(jnp.exp(-jnp.abs(s)))
    log_beta = jnp.minimum(s, 0.0) - t
    log_keep = log_beta - s
    if valid is not None:
        log_keep = jnp.where(valid, log_keep, 0.0)
    return log_beta, log_keep


def _sb_kernel(q_ref, k_ref, v_ref, o_ref, k2_scr, v2_scr, q2_scr, carry_scr, acc_scr, *, tq):
    i = pl.program_id(1)
    seq = k_ref.shape[1]
    half = SB_HEAD_DIM
    hrows = 2 * tq
    rows = SB_KV_HEADS * hrows

    @pl.when(i == 0)
    def _():
        lane = lax.broadcasted_iota(jnp.int32, (seq, LANES), 1)
        for p in range(SB_KV_HEADS // 2):
            for src, dst in ((k_ref, k2_scr), (v_ref, v2_scr)):
                pair = src[0, :, p * LANES:(p + 1) * LANES]
                swapped = pltpu.roll(pair, half, 1)
                dst[2 * p] = jnp.where(lane < half, pair, swapped).astype(BF16)
                dst[2 * p + 1] = jnp.where(lane < half, swapped, pair).astype(BF16)

    lane_q = lax.broadcasted_iota(jnp.int32, (tq, LANES), 1)
    zero = jnp.zeros((), BF16)
    for kh in range(SB_KV_HEADS):
        qg = q_ref[0, :, kh * LANES:(kh + 1) * LANES] * jnp.asarray(SB_HEAD_DIM ** -0.5, BF16)
        q2_scr[kh * hrows:kh * hrows + tq, :] = jnp.where(lane_q < half, qg, zero)
        q2_scr[kh * hrows + tq:(kh + 1) * hrows, :] = jnp.where(lane_q >= half, qg, zero)

    tmat = _later_mask(tq)

    def tile(j, carry, acc, valid):
        start = pl.multiple_of(j * tq, tq)
        s = jnp.concatenate([_dot_nt(q2_scr[kh * hrows:(kh + 1) * hrows, :], k2_scr[kh, pl.ds(start, tq), :])
                             for kh in range(SB_KV_HEADS)], axis=0)
        log_beta, log_keep = _sb_logs(s, valid)
        w = jnp.exp(log_beta + _dot_ones(log_keep, tmat) + carry)
        if valid is not None:
            w = jnp.where(valid, w, 0.0)
        w = w.astype(BF16)
        pv = jnp.concatenate([_dot(w[kh * hrows:(kh + 1) * hrows], v2_scr[kh, pl.ds(start, tq), :])
                              for kh in range(SB_KV_HEADS)], axis=0)
        return carry + jnp.sum(log_keep, axis=-1, keepdims=True), acc + pv

    row = lax.broadcasted_iota(jnp.int32, (rows, tq), 0)
    col = lax.broadcasted_iota(jnp.int32, (rows, tq), 1)
    carry, acc = tile(i, jnp.zeros((rows, LANES), F32), jnp.zeros((rows, LANES), F32), col < row % tq)
    carry_scr[...] = carry
    acc_scr[...] = acc

    def cond(st):
        j, live = st
        return jnp.logical_and(j >= 0, live > SB_DEAD_LOG)

    def body(st):
        j, _ = st
        carry, acc = tile(j, carry_scr[...], acc_scr[...], None)
        carry_scr[...] = carry
        acc_scr[...] = acc
        return j - 1, jnp.max(carry)

    lax.while_loop(cond, body, (i - 1, jnp.max(carry)))
    for kh in range(SB_KV_HEADS):
        o_ref[0, :, kh * LANES:(kh + 1) * LANES] = jnp.where(
            lane_q < half, acc_scr[kh * hrows:kh * hrows + tq, :],
            acc_scr[kh * hrows + tq:(kh + 1) * hrows, :]).astype(o_ref.dtype)


def sb_attention(q, k, v, tq=128):
    bsz, seq, _ = q.shape
    rows = SB_KV_HEADS * 2 * tq
    return pl.pallas_call(
        functools.partial(_sb_kernel, tq=tq),
        grid=(bsz, seq // tq),
        in_specs=[pl.BlockSpec((1, tq, SB_Q_DIM), lambda b, i: (b, i, 0)),
                  pl.BlockSpec((1, seq, SB_KV_DIM), lambda b, i: (b, 0, 0)),
                  pl.BlockSpec((1, seq, SB_KV_DIM), lambda b, i: (b, 0, 0))],
        out_specs=pl.BlockSpec((1, tq, SB_Q_DIM), lambda b, i: (b, i, 0)),
        out_shape=jax.ShapeDtypeStruct((bsz, seq, SB_Q_DIM), BF16),
        scratch_shapes=[pltpu.VMEM((SB_KV_HEADS, seq, LANES), BF16),
                        pltpu.VMEM((SB_KV_HEADS, seq, LANES), BF16),
                        pltpu.VMEM((rows, LANES), BF16),
                        pltpu.VMEM((rows, LANES), F32),
                        pltpu.VMEM((rows, LANES), F32)],
        compiler_params=_cparams("parallel", "arbitrary"), name="sb_attention")(q, k, v)


def _sb_pages(q2, kts, vts, tmat, carry, acc, valid):
    rows = q2.shape[0]
    s = jnp.concatenate([_dot(q2, kt) for kt in kts], axis=0)
    log_beta, log_keep = _sb_logs(s, valid)
    later = _dot_ones(log_keep, tmat)
    total = jnp.sum(log_keep, axis=-1, keepdims=True)
    carries = []
    for u in range(len(kts)):
        carries.append(carry)
        carry = carry + total[u * rows:(u + 1) * rows]
    w = jnp.exp(log_beta + later + jnp.concatenate(carries, axis=0))
    if valid is not None:
        w = jnp.where(valid, w, 0.0)
    w = w.astype(BF16)
    for u, vt in enumerate(vts):
        acc = acc + _dot_nt(w[u * rows:(u + 1) * rows], vt)
    return carry, acc


def _sb_dec_kernel(pt_ref, q_ref, kn_ref, vn_ref, *rest, pp, lq):
    del pt_ref
    k_refs, v_refs = rest[:pp], rest[pp:2 * pp]
    o_ref, carry_scr, acc_scr = rest[2 * pp:]
    s = pl.program_id(1)
    q2 = q_ref[0]
    rows = q2.shape[0]
    tmat = _later_mask(PAGE_SIZE)

    @pl.when(s == 0)
    def _():
        r = lax.broadcasted_iota(jnp.int32, (rows, PAGE_SIZE), 0)
        cidx = lax.broadcasted_iota(jnp.int32, (rows, PAGE_SIZE), 1)
        valid = cidx < r % lq
        carry, acc = _sb_pages(q2, [kn_ref[0].astype(BF16)], [vn_ref[0].astype(BF16)], tmat,
                               jnp.zeros((rows, LANES), F32), jnp.zeros((rows, SB_KV_DIM), F32), valid)
        carry_scr[...] = carry
        acc_scr[...] = acc

    carry, acc = carry_scr[...], acc_scr[...]
    per = pp // min(DECODE_PAGE_GROUPS, pp)
    for g0 in range(0, pp, per):
        carry, acc = _sb_pages(q2, [r[0].astype(BF16) for r in k_refs[g0:g0 + per]],
                               [r[0].astype(BF16) for r in v_refs[g0:g0 + per]], tmat, carry, acc, None)
    carry_scr[...] = carry
    acc_scr[...] = acc

    @pl.when(s == pl.num_programs(1) - 1)
    def _():
        o_ref[0] = acc


def _pages_per_step(n_pages):
    pp = DECODE_PAGES_PER_STEP
    while n_pages % pp:
        pp //= 2
    return pp


def sb_decode(q_bd, kt_new, vt_new, kt_pool, vt_pool, page_table, lq):
    bsz, rows, _ = q_bd.shape
    n_pages = page_table.shape[1]
    pp = _pages_per_step(n_pages)

    def page_map(u):
        return lambda b, s, pt: (pt[b, n_pages - 1 - (s * pp + u)], 0, 0)

    per_b = lambda b, s, pt: (b, 0, 0)
    page_spec = [pl.BlockSpec((1, SB_KV_DIM, PAGE_SIZE), page_map(u)) for u in range(pp)]
    grid_spec = pltpu.PrefetchScalarGridSpec(
        num_scalar_prefetch=1, grid=(bsz, n_pages // pp),
        in_specs=[pl.BlockSpec((1, rows, SB_KV_DIM), per_b),
                  pl.BlockSpec((1, SB_KV_DIM, PAGE_SIZE), per_b),
                  pl.BlockSpec((1, SB_KV_DIM, PAGE_SIZE), per_b)] + page_spec + page_spec,
        out_specs=pl.BlockSpec((1, rows, SB_KV_DIM), per_b),
        scratch_shapes=[pltpu.VMEM((rows, LANES), F32), pltpu.VMEM((rows, SB_KV_DIM), F32)])
    return pl.pallas_call(
        functools.partial(_sb_dec_kernel, pp=pp, lq=lq), grid_spec=grid_spec,
        out_shape=jax.ShapeDtypeStruct((bsz, rows, SB_KV_DIM), F32),
        compiler_params=_cparams("parallel", "arbitrary"), name="sb_decode",
    )(page_table, q_bd, kt_new, vt_new, *([kt_pool] * pp), *([vt_pool] * pp))


def _hgrn_kernel(hq_ref, hf_ref, hi_ref, hg_ref, s0_ref, lb_ref, nw_ref, o_ref, sfin_ref, st_scr, o_scr, *, tl, lv):
    c = pl.program_id(1)
    nchunk = tl // HG_CHUNK

    @pl.when(c == 0)
    def _():
        for h in range(HG_HEADS):
            st_scr[h] = s0_ref[0, h].T

    lb = lb_ref[...]
    f_pre = hf_ref[0]
    sig = _sigmoid(f_pre)
    logf = jnp.log(lb + (1.0 - lb) * sig)
    kk = (1.0 - lb) * (1.0 - sig)
    if lv < tl:
        live = lax.broadcasted_iota(jnp.int32, logf.shape, 0) < lv
        logf = jnp.where(live, logf, 0.0)
        kk = jnp.where(live, kk, 0.0)
    row = lax.broadcasted_iota(jnp.int32, (tl, tl), 0)
    col = lax.broadcasted_iota(jnp.int32, (tl, tl), 1)
    same = (row // HG_CHUNK) == (col // HG_CHUNK)
    causal = jnp.logical_and(same, row >= col)
    bcum = _ones_dot(causal.astype(BF16), logf)
    btot = _ones_dot(same.astype(BF16), logf)
    q_dec = (hq_ref[0] * jnp.exp(bcum)).astype(BF16)
    k_inv = (kk * jnp.exp(-bcum)).astype(BF16)
    k_end = kk * jnp.exp(btot - bcum)
    e_tot = jnp.exp(btot)
    v = hi_ref[0].astype(BF16)
    chunk_of_row = lax.broadcasted_iota(jnp.int32, (tl, HG_DK), 0) // HG_CHUNK

    for h in range(HG_HEADS):
        hs = slice(h * HG_DK, (h + 1) * HG_DK)
        att = jnp.where(causal, _dot_nt(q_dec[:, hs], k_inv[:, hs]), 0.0)
        o_scr[:, hs] = _dot(att.astype(BF16), v[:, hs])
        k_blocks = jnp.concatenate([jnp.where(chunk_of_row == cc, k_end[:, hs], 0.0) for cc in range(nchunk)], axis=1)
        incr = _dot_tn(v[:, hs], k_blocks.astype(BF16))
        st = st_scr[h]
        for cc in range(nchunk):
            rs = slice(cc * HG_CHUNK, (cc + 1) * HG_CHUNK)
            o_scr[rs, hs] += _dot_nt(q_dec[rs, hs], st.astype(BF16))
            decay = e_tot[cc * HG_CHUNK:cc * HG_CHUNK + 1, hs]
            st = st * decay + incr[:, cc * HG_DK:(cc + 1) * HG_DK]
        st_scr[h] = st

    o = o_scr[...]
    gate = _silu(hg_ref[0])
    for h in range(HG_HEADS):
        hs = slice(h * HG_DV, (h + 1) * HG_DV)
        o_ref[0, :, hs] = (_rms(o[:, hs], nw_ref[:, hs]) * gate[:, hs]).astype(o_ref.dtype)

    @pl.when(c == pl.num_programs(1) - 1)
    def _():
        for h in range(HG_HEADS):
            sfin_ref[0, h] = st_scr[h].T


def hgrn_mixer(hq, hf, hi, hg, s0, lb, norm_w, tl, lv, out_dtype):
    bsz, seq, _ = hq.shape
    row3 = lambda b, c: (b, c, 0)
    full2 = lambda b, c: (0, 0)
    st4 = lambda b, c: (b, 0, 0, 0)
    return pl.pallas_call(
        functools.partial(_hgrn_kernel, tl=tl, lv=lv),
        grid=(bsz, seq // tl),
        in_specs=[pl.BlockSpec((1, tl, HG_DIM), row3)] * 4 + [
            pl.BlockSpec((1, HG_HEADS, HG_DK, HG_DV), st4),
            pl.BlockSpec((1, HG_DIM), full2), pl.BlockSpec((1, HG_DIM), full2)],
        out_specs=[pl.BlockSpec((1, tl, HG_DIM), row3), pl.BlockSpec((1, HG_HEADS, HG_DK, HG_DV), st4)],
        out_shape=[jax.ShapeDtypeStruct((bsz, seq, HG_DIM), out_dtype),
                   jax.ShapeDtypeStruct((bsz, HG_HEADS, HG_DK, HG_DV), F32)],
        scratch_shapes=[pltpu.VMEM((HG_HEADS, HG_DV, HG_DK), F32), pltpu.VMEM((tl, HG_DIM), F32)],
        compiler_params=_cparams("parallel", "arbitrary"), name="hgrn_mixer",
    )(hq, hf, hi, hg, s0, lb, norm_w)


def _mla_pre_kernel(cq_ref, ckv_ref, kra_ref, krb_ref, cosq_ref, sinq_ref, cosk_ref, sink_ref, qnw_ref, kvnw_ref,
                    wn_ref, wr_ref, wrs_ref, wukt_ref, ql_ref, qr_ref, c_ref, r_ref):
    n = _rms(cq_ref[...], qnw_ref[...]).astype(BF16)
    q_nope = _dot(n, wn_ref[...]).astype(BF16)
    q_rope = _dot(n, wr_ref[...]) * cosq_ref[...] + _dot(n, wrs_ref[...]) * sinq_ref[...]
    qr_ref[...] = q_rope.astype(qr_ref.dtype)
    for h in range(MLA_HEADS):
        ql_ref[:, h * MLA_KV_LORA:(h + 1) * MLA_KV_LORA] = _dot(
            q_nope[:, h * MLA_NOPE:(h + 1) * MLA_NOPE], wukt_ref[h]).astype(ql_ref.dtype)
    c_ref[...] = _rms(ckv_ref[...], kvnw_ref[...])
    r = kra_ref[...] * cosk_ref[...] + krb_ref[...] * sink_ref[...]
    r_ref[...] = r[:, :MLA_ROPE]


def mla_pre(cq, ckv, kra, krb, tabs, qnw, kvnw, wn, wr, wrs, wukt, tm=512):
    m = cq.shape[0]
    tm = min(tm, m, tabs[0].shape[0])
    nblk = tabs[0].shape[0] // tm
    rowb = lambda i: (i, 0)
    tabb = lambda i: (i % nblk, 0)
    full2 = lambda i: (0, 0)
    hq = MLA_HEADS * MLA_ROPE
    return pl.pallas_call(
        _mla_pre_kernel,
        grid=(m // tm,),
        in_specs=[pl.BlockSpec((tm, MLA_Q_LORA), rowb), pl.BlockSpec((tm, MLA_KV_LORA), rowb),
                  pl.BlockSpec((tm, LANES), rowb), pl.BlockSpec((tm, LANES), rowb),
                  pl.BlockSpec((tm, hq), tabb), pl.BlockSpec((tm, hq), tabb),
                  pl.BlockSpec((tm, LANES), tabb), pl.BlockSpec((tm, LANES), tabb),
                  pl.BlockSpec((1, MLA_Q_LORA), full2), pl.BlockSpec((1, MLA_KV_LORA), full2),
                  pl.BlockSpec(wn.shape, full2), pl.BlockSpec(wr.shape, full2), pl.BlockSpec(wrs.shape, full2),
                  pl.BlockSpec(wukt.shape, lambda i: (0, 0, 0))],
        out_specs=[pl.BlockSpec((tm, MLA_HEADS * MLA_KV_LORA), rowb), pl.BlockSpec((tm, hq), rowb),
                   pl.BlockSpec((tm, MLA_KV_LORA), rowb), pl.BlockSpec((tm, MLA_ROPE), rowb)],
        out_shape=[jax.ShapeDtypeStruct((m, MLA_HEADS * MLA_KV_LORA), BF16), jax.ShapeDtypeStruct((m, hq), BF16),
                   jax.ShapeDtypeStruct((m, MLA_KV_LORA), F32), jax.ShapeDtypeStruct((m, MLA_ROPE), F32)],
        compiler_params=_cparams("parallel"), name="mla_pre",
    )(cq, ckv, kra, krb, *tabs, qnw, kvnw, wn, wr, wrs, wukt)


def _mla_softmax_step(s_list, cblks, m, l, acc):
    def fold(x, op):
        parts = [x[:, b * LANES:(b + 1) * LANES] for b in range(x.shape[1] // LANES)]
        return functools.reduce(op, parts)

    smax = functools.reduce(jnp.maximum, [fold(s, jnp.maximum) for s in s_list])
    m_new = jnp.maximum(m, jnp.max(smax, axis=-1, keepdims=True))
    alpha = jnp.exp(m - m_new)
    l = alpha * l
    acc = jnp.concatenate([alpha] * (acc.shape[1] // LANES), axis=1) * acc
    for s, cblk in zip(s_list, cblks):
        p = jnp.exp(s - jnp.concatenate([m_new] * (s.shape[1] // LANES), axis=1))
        l = l + fold(p, jnp.add)
        acc = acc + _dot(p.astype(BF16), cblk)
    return m_new, l, acc


def _mla_kernel(ql_ref, qr_ref, c_ref, r_ref, wuv_ref, o_ref, qs_scr, qrs_scr, cb_scr, rb_scr, m_scr, l_scr, acc_scr,
                *, tq, tk):
    i = pl.program_id(1)

    @pl.when(i == 0)
    def _():
        cb_scr[...] = c_ref[0].astype(BF16)
        rb_scr[...] = r_ref[0].astype(BF16)

    for h in range(MLA_HEADS):
        qs_scr[h * tq:(h + 1) * tq, :] = ql_ref[0, :, h * MLA_KV_LORA:(h + 1) * MLA_KV_LORA]
        qrs_scr[h * tq:(h + 1) * tq, :] = qr_ref[0, :, h * MLA_ROPE:(h + 1) * MLA_ROPE]
    rows = MLA_HEADS * tq
    grows = rows // MLA_ROW_GROUPS

    def step(j, valid):
        start = pl.multiple_of(j * tk, tk)
        cblk = cb_scr[pl.ds(start, tk), :]
        rblk = rb_scr[pl.ds(start, tk), :]
        for g in range(MLA_ROW_GROUPS):
            rs = slice(g * grows, (g + 1) * grows)
            s = (_dot_nt(qs_scr[rs, :], cblk) + _dot_nt(qrs_scr[rs, :], rblk)) * MLA_SCALE
            if valid is None:
                m, l, acc = m_scr[rs, :], l_scr[rs, :], acc_scr[rs, :]
            else:
                s = jnp.where(valid, s, NEG_BIG)
                m, l = jnp.full((grows, LANES), NEG_BIG, F32), jnp.zeros((grows, LANES), F32)
                acc = jnp.zeros((grows, MLA_KV_LORA), F32)
            m_scr[rs, :], l_scr[rs, :], acc_scr[rs, :] = _mla_softmax_step([s], [cblk], m, l, acc)

    nfull = (i * tq) // tk
    q_idx = i * tq + lax.broadcasted_iota(jnp.int32, (grows, tk), 0) % tq
    k_idx = nfull * tk + lax.broadcasted_iota(jnp.int32, (grows, tk), 1)
    step(nfull, k_idx <= q_idx)

    def body(j, carry):
        step(j, None)
        return carry

    lax.fori_loop(0, nfull, body, 0)
    ctx = (acc_scr[...] / jnp.sum(l_scr[...], axis=-1, keepdims=True)).astype(BF16)
    for h in range(MLA_HEADS):
        o_ref[0, :, h * MLA_V:(h + 1) * MLA_V] = _dot(ctx[h * tq:(h + 1) * tq], wuv_ref[h]).astype(o_ref.dtype)


def mla_attention(ql, qr, c, r, wuv, tq=128, tk=512):
    bsz, seq, _ = ql.shape
    tk = min(tk, seq)
    rows = MLA_HEADS * tq
    return pl.pallas_call(
        functools.partial(_mla_kernel, tq=tq, tk=tk),
        grid=(bsz, seq // tq),
        in_specs=[pl.BlockSpec((1, tq, MLA_HEADS * MLA_KV_LORA), lambda b, i: (b, i, 0)),
                  pl.BlockSpec((1, tq, MLA_HEADS * MLA_ROPE), lambda b, i: (b, i, 0)),
                  pl.BlockSpec((1, seq, MLA_KV_LORA), lambda b, i: (b, 0, 0)),
                  pl.BlockSpec((1, seq, MLA_ROPE), lambda b, i: (b, 0, 0)),
                  pl.BlockSpec(wuv.shape, lambda b, i: (0, 0, 0))],
        out_specs=pl.BlockSpec((1, tq, MLA_HEADS * MLA_V), lambda b, i: (b, i, 0)),
        out_shape=jax.ShapeDtypeStruct((bsz, seq, MLA_HEADS * MLA_V), BF16),
        scratch_shapes=[pltpu.VMEM((rows, MLA_KV_LORA), BF16), pltpu.VMEM((rows, MLA_ROPE), BF16),
                        pltpu.VMEM((seq, MLA_KV_LORA), BF16), pltpu.VMEM((seq, MLA_ROPE), BF16),
                        pltpu.VMEM((rows, LANES), F32), pltpu.VMEM((rows, LANES), F32),
                        pltpu.VMEM((rows, MLA_KV_LORA), F32)],
        compiler_params=_cparams("parallel", "arbitrary"), name="mla_attention")(ql, qr, c, r, wuv)


def _mla_pages(qs, qrs, cs, rts, m, l, acc, valid):
    s_list = [(_dot_nt(qs, c) + _dot(qrs, rt)) * MLA_SCALE for c, rt in zip(cs, rts)]
    if valid is not None:
        s_list = [jnp.where(valid, s, NEG_BIG) for s in s_list]
    return _mla_softmax_step(s_list, cs, m, l, acc)


def _mla_dec_kernel(pt_ref, ql_ref, qr_ref, cn_ref, rn_ref, *rest, pp, lq):
    del pt_ref
    c_refs, r_refs = rest[:pp], rest[pp:2 * pp]
    o_ref, m_scr, l_scr, acc_scr = rest[2 * pp:]
    s = pl.program_id(1)
    qs = ql_ref[0]
    qrs = qr_ref[0]
    rows = qs.shape[0]

    @pl.when(s == 0)
    def _():
        r2 = lax.broadcasted_iota(jnp.int32, (rows, PAGE_SIZE), 0)
        c2 = lax.broadcasted_iota(jnp.int32, (rows, PAGE_SIZE), 1)
        m, l, acc = _mla_pages(qs, qrs, [cn_ref[0].astype(BF16)], [rn_ref[0].astype(BF16)],
                               jnp.full((rows, LANES), NEG_BIG, F32), jnp.zeros((rows, LANES), F32),
                               jnp.zeros((rows, MLA_KV_LORA), F32), c2 <= r2 % lq)
        m_scr[...] = m
        l_scr[...] = l
        acc_scr[...] = acc

    m, l, acc = m_scr[...], l_scr[...], acc_scr[...]
    per = pp // min(DECODE_PAGE_GROUPS, pp)
    for g0 in range(0, pp, per):
        m, l, acc = _mla_pages(qs, qrs, [r[0].astype(BF16) for r in c_refs[g0:g0 + per]],
                               [r[0].astype(BF16) for r in r_refs[g0:g0 + per]], m, l, acc, None)
    m_scr[...] = m
    l_scr[...] = l
    acc_scr[...] = acc

    @pl.when(s == pl.num_programs(1) - 1)
    def _():
        o_ref[0] = acc / jnp.sum(l, axis=-1, keepdims=True)


def mla_decode(ql, qr, c_new, rt_new, c_pool, rt_pool, page_table, lq):
    bsz, rows, _ = ql.shape
    n_pages = page_table.shape[1]
    pp = _pages_per_step(n_pages)

    def page_map(u):
        return lambda b, s, pt: (pt[b, s * pp + u], 0, 0)

    per_b = lambda b, s, pt: (b, 0, 0)
    grid_spec = pltpu.PrefetchScalarGridSpec(
        num_scalar_prefetch=1, grid=(bsz, n_pages // pp),
        in_specs=[pl.BlockSpec((1, rows, MLA_KV_LORA), per_b), pl.BlockSpec((1, rows, MLA_ROPE), per_b),
                  pl.BlockSpec((1, PAGE_SIZE, MLA_KV_LORA), per_b), pl.BlockSpec((1, MLA_ROPE, PAGE_SIZE), per_b)]
        + [pl.BlockSpec((1, PAGE_SIZE, MLA_KV_LORA), page_map(u)) for u in range(pp)]
        + [pl.BlockSpec((1, MLA_ROPE, PAGE_SIZE), page_map(u)) for u in range(pp)],
        out_specs=pl.BlockSpec((1, rows, MLA_KV_LORA), per_b),
        scratch_shapes=[pltpu.VMEM((rows, LANES), F32), pltpu.VMEM((rows, LANES), F32),
                        pltpu.VMEM((rows, MLA_KV_LORA), F32)])
    return pl.pallas_call(
        functools.partial(_mla_dec_kernel, pp=pp, lq=lq), grid_spec=grid_spec,
        out_shape=jax.ShapeDtypeStruct((bsz, rows, MLA_KV_LORA), F32),
        compiler_params=_cparams("parallel", "arbitrary"), name="mla_decode",
    )(page_table, ql, qr, c_new, rt_new, *([c_pool] * pp), *([rt_pool] * pp))


def _head_linear_kernel(x_ref, w_ref, o_ref):
    o_ref[0] = _dot(x_ref[0].astype(BF16), w_ref[0])


def head_linear(x, w):
    nh, m, k = x.shape
    n = w.shape[2]
    return pl.pallas_call(
        _head_linear_kernel, grid=(nh,),
        in_specs=[pl.BlockSpec((1, m, k), lambda h: (h, 0, 0)), pl.BlockSpec((1, k, n), lambda h: (h, 0, 0))],
        out_specs=pl.BlockSpec((1, m, n), lambda h: (h, 0, 0)),
        out_shape=jax.ShapeDtypeStruct((nh, m, n), F32),
        compiler_params=_cparams("parallel"), name="head_linear")(x, w)


def _pad_cols(w, n):
    return jnp.pad(w, ((0, 0), (0, n - w.shape[1])))


def _lane_row(v):
    return jnp.pad(v.astype(F32), (0, LANES - v.shape[0]))[None, :]


def _rope_tables(pos):
    half = MLA_ROPE // 2
    inv = ROPE_THETA ** (-jnp.arange(half, dtype=F32) / half)
    ang = pos[:, None] * inv[None, :]
    cos = jnp.concatenate([jnp.cos(ang), jnp.cos(ang)], axis=-1)
    sin = jnp.concatenate([-jnp.sin(ang), jnp.sin(ang)], axis=-1)
    cosq, sinq = jnp.tile(cos, (1, MLA_HEADS)), jnp.tile(sin, (1, MLA_HEADS))
    cosk, sink = _pad_cols(cos, LANES), _pad_cols(sin, LANES)
    return cosq, sinq, cosk, sink


def _swap_halves(w):
    half = MLA_ROPE // 2
    g = w.reshape(w.shape[0], -1, 2, half)
    return g[:, :, ::-1, :].reshape(w.shape)


def _even_weights(w_in, w_out):
    cuts = np.cumsum([SSM_INNER, SSM_CONV_DIM, SSM_HEADS, SB_Q_DIM, SB_KV_DIM, SB_KV_DIM])[:-1].tolist()
    wz, wxbc, wdt, wq, wk, wv = jnp.split(w_in.astype(BF16), cuts, axis=1)
    wo = w_out.astype(BF16)
    return (wz, wxbc, _pad_cols(wdt, LANES), wq, wk, wv), (wo[:SSM_INNER], wo[SSM_INNER:])


def _odd_weights(w_in, w_out):
    cuts = np.cumsum([HG_DIM, HG_DIM, HG_DIM, HG_DIM, MLA_Q_LORA, MLA_KV_LORA, MLA_ROPE])[:-1].tolist()
    whq, whf, whi, whg, wcq, wckv, wkr = jnp.split(w_in.astype(BF16), cuts, axis=1)
    wo = w_out.astype(BF16)
    ws = (whq, whf, whi, whg, wcq, wckv, _pad_cols(wkr, LANES), _pad_cols(_swap_halves(wkr), LANES))
    return ws, (wo[:HG_DIM], wo[HG_DIM:])


def _pad_rows(a, n):
    return jnp.pad(a, ((0, 0), (0, n - a.shape[1]), (0, 0)))


def _transposed_pages(pool):
    npool = pool.shape[0]
    return jnp.transpose(pool, (0, 2, 3, 1)).reshape(npool, -1, PAGE_SIZE)


def _even_layer(x, sample, page_table, w_in, w_out, conv_w, conv_b, dt_bias, a_log, d_skip, norm_w,
                conv0, s0, k_pool, v_pool):
    bsz, seq, d = x.shape
    ws, (wo_a, wo_b) = _even_weights(w_in, w_out)
    z, xbc, dt, q, k, v = multi_linear(x.reshape(bsz * seq, d), ws, (F32, F32, F32, BF16, F32, F32))
    xbc3, z3, dt3 = (a.reshape(bsz, seq, -1) for a in (xbc, z, dt))
    conv_new = jnp.concatenate([conv0, xbc3], axis=1)[:, -(SSM_CONV - 1):]
    lv = seq
    if seq < SSM_CHUNK:
        lv = seq
        xbc3, z3, dt3 = (_pad_rows(a, SSM_CHUNK) for a in (xbc3, z3, dt3))
    y_a, s_new = ssd_mixer(xbc3, z3, dt3, conv0, s0, conv_w, conv_b[None, :], _lane_row(dt_bias), _lane_row(a_log),
                           jnp.repeat(d_skip.astype(F32), SSM_HEAD_DIM)[None, :], norm_w[None, :],
                           min(lv, SSM_CHUNK), F32 if sample else BF16)
    y_a = y_a[:, :seq].reshape(bsz * seq, SSM_INNER)
    q3 = q.reshape(bsz, seq, SB_Q_DIM)
    k3, v3 = k.reshape(bsz, seq, SB_KV_DIM), v.reshape(bsz, seq, SB_KV_DIM)
    if sample:
        group = SB_HEADS // SB_KV_HEADS
        q5 = (q3 * jnp.asarray(SB_HEAD_DIM ** -0.5, BF16)).reshape(bsz, seq, SB_KV_HEADS, group, SB_HEAD_DIM)
        q5 = jnp.transpose(q5, (0, 2, 3, 1, 4)).reshape(bsz, SB_KV_HEADS, group * seq, SB_HEAD_DIM)
        q_bd = jnp.einsum('bkxd,kj->bkxjd', q5, jnp.eye(SB_KV_HEADS, dtype=BF16))
        q_bd = q_bd.reshape(bsz, SB_KV_HEADS * group * seq, SB_KV_DIM)
        tr = lambda a: jnp.transpose(_pad_rows(a, PAGE_SIZE), (0, 2, 1))
        acc = sb_decode(q_bd, tr(k3), tr(v3), _transposed_pages(k_pool), _transposed_pages(v_pool), page_table, seq)
        acc = acc.reshape(bsz, SB_KV_HEADS, group, seq, SB_KV_HEADS, SB_HEAD_DIM)
        y_b = jnp.einsum('bkrtkd->btkrd', acc).reshape(bsz * seq, SB_Q_DIM)
    else:
        y_b = sb_attention(q3, k3, v3).reshape(bsz * seq, SB_Q_DIM)
    return (y_a, y_b), (wo_a, wo_b), s_new, conv_new, k3.reshape(bsz, seq, SB_KV_HEADS, SB_HEAD_DIM), \
        v3.reshape(bsz, seq, SB_KV_HEADS, SB_HEAD_DIM)


def _odd_layer(x, sample, pos_start, page_table, lb, w_in, w_out, hg_norm_w, q_norm_w, kv_norm_w, w_uq, w_uk, w_uv,
               s0, c_pool, r_pool):
    bsz, seq, d = x.shape
    m = bsz * seq
    ws, (wo_c, wo_d) = _odd_weights(w_in, w_out)
    hq, hf, hi, hg, cq, ckv, kra, krb = multi_linear(x.reshape(m, d), ws, (F32,) * 8)
    h3 = [a.reshape(bsz, seq, HG_DIM) for a in (hq, hf, hi, hg)]
    tl = min(256, seq)
    lv = tl
    if seq < HG_CHUNK:
        tl, lv = HG_CHUNK, seq
        h3 = [_pad_rows(a, HG_CHUNK) for a in h3]
    o_c, s_new = hgrn_mixer(*h3, s0, lb[None, :].astype(F32), hg_norm_w.reshape(1, HG_DIM), tl, lv,
                            F32 if sample else BF16)
    o_c = o_c[:, :seq].reshape(m, HG_DIM)
    pos = jnp.arange(seq, dtype=F32) + pos_start
    tabs = _rope_tables(pos)
    if sample:
        tabs = tuple(jnp.tile(t, (bsz, 1)) for t in tabs)
    w_uq_b = w_uq.astype(BF16)
    wn = w_uq_b[:, :, :MLA_NOPE].reshape(MLA_Q_LORA, MLA_HEADS * MLA_NOPE)
    wr = w_uq_b[:, :, MLA_NOPE:].reshape(MLA_Q_LORA, MLA_HEADS * MLA_ROPE)
    wukt = jnp.transpose(w_uk.astype(BF16), (1, 2, 0))
    wuv = jnp.transpose(w_uv.astype(BF16), (1, 0, 2))
    ql, qr, c_new, r_new = mla_pre(cq, ckv, kra, krb, tabs, q_norm_w[None, :], kv_norm_w[None, :],
                                   wn, wr, _swap_halves(wr), wukt)
    c3, r3 = c_new.reshape(bsz, seq, MLA_KV_LORA), r_new.reshape(bsz, seq, MLA_ROPE)
    if sample:
        ql3 = jnp.transpose(ql.reshape(bsz, seq, MLA_HEADS, MLA_KV_LORA), (0, 2, 1, 3))
        qr3 = jnp.transpose(qr.reshape(bsz, seq, MLA_HEADS, MLA_ROPE), (0, 2, 1, 3))
        ctx = mla_decode(ql3.reshape(bsz, MLA_HEADS * seq, MLA_KV_LORA), qr3.reshape(bsz, MLA_HEADS * seq, MLA_ROPE),
                         _pad_rows(c3, PAGE_SIZE), jnp.transpose(_pad_rows(r3, PAGE_SIZE), (0, 2, 1)), c_pool,
                         jnp.transpose(r_pool, (0, 2, 1)), page_table, seq)
        ctx = jnp.transpose(ctx.reshape(bsz, MLA_HEADS, seq, MLA_KV_LORA), (1, 0, 2, 3))
        o_d = head_linear(ctx.reshape(MLA_HEADS, m, MLA_KV_LORA), wuv)
        o_d = jnp.transpose(o_d, (1, 0, 2)).reshape(m, MLA_HEADS * MLA_V)
    else:
        o_d = mla_attention(ql.reshape(bsz, seq, -1), qr.reshape(bsz, seq, -1), c3, r3, wuv)
        o_d = o_d.reshape(m, MLA_HEADS * MLA_V)
    return (o_c, o_d), (wo_c, wo_d), s_new, c3, r3


def _hgrn_lower_bound(gamma, layer):
    p = jax.nn.softmax(gamma.astype(F32), axis=0)
    return jnp.cumsum(p, axis=0)[layer] - p[0]


def kernel(x_prompt, x_sample, mem_prompt, state_ssm, state_conv, cache_sb_k, cache_sb_v, state_hgrn,
           cache_mla_latent, cache_mla_rope, cache_mem_k, cache_mem_v, page_table,
           w_in_even, w_out_even, ssm_conv_w, ssm_conv_b, ssm_dt_bias, ssm_a_log, ssm_d, ssm_norm_w,
           w_in_odd, w_out_odd, hg_lower_bound, hg_norm_w, mla_q_norm_w, mla_kv_norm_w, mla_w_uq, mla_w_uk, mla_w_uv,
           w_mem_q, w_mem_k, w_mem_v, w_mem_o, ffn_w_gate, ffn_w_up, ffn_w_down, ln_g, ln_b):
    past_len = page_table.shape[1] * PAGE_SIZE
    mem_len = mem_prompt.shape[1]

    def run(x, sample):
        bsz, seq, d = x.shape
        m = bsz * seq
        pos_start = past_len if sample else 0
        outs = dict(ssm=[], conv=[], sbk=[], sbv=[], hg=[], lat=[], rope=[], mk=[], mv=[])
        x2 = x.reshape(m, d)
        for l in range(DEPTH):
            i = l // 2
            if l % 2 == 0:
                if sample:
                    conv0, s0 = state_conv[i], state_ssm[i]
                else:
                    conv0 = jnp.zeros((bsz, SSM_CONV - 1, SSM_CONV_DIM), F32)
                    s0 = jnp.zeros((bsz, SSM_HEADS, SSM_HEAD_DIM, SSM_STATE), F32)
                ys, wos, s1, conv1, k_new, v_new = _even_layer(
                    x2.reshape(bsz, seq, d), sample, page_table, w_in_even[i], w_out_even[i], ssm_conv_w[i],
                    ssm_conv_b[i], ssm_dt_bias[i], ssm_a_log[i], ssm_d[i], ssm_norm_w[i], conv0, s0,
                    cache_sb_k[i], cache_sb_v[i])
                outs['ssm'].append(s1)
                outs['conv'].append(conv1)
                outs['sbk'].append(k_new)
                outs['sbv'].append(v_new)
            else:
                s0 = state_hgrn[i] if sample else jnp.zeros((bsz, HG_HEADS, HG_DK, HG_DV), F32)
                ys, wos, s1, c_new, r_new = _odd_layer(
                    x2.reshape(bsz, seq, d), sample, pos_start, page_table, _hgrn_lower_bound(hg_lower_bound, l),
                    w_in_odd[i], w_out_odd[i], hg_norm_w[i], mla_q_norm_w[i], mla_kv_norm_w[i], mla_w_uq[i],
                    mla_w_uk[i], mla_w_uv[i], s0, cache_mla_latent[i], cache_mla_rope[i])
                outs['hg'].append(s1)
                outs['lat'].append(c_new)
                outs['rope'].append(r_new)
            x2 = proj_ln(list(ys), list(wos), x2, ln_g[l, 0][None, :], ln_b[l, 0][None, :])
            if sample:
                mk = cache_mem_k[l].reshape(bsz, mem_len, MEM_INNER)
                mv = cache_mem_v[l].reshape(bsz, mem_len, MEM_INNER)
                bb, tl = 8, seq
            else:
                mk, mv = multi_linear(mem_prompt.reshape(bsz * mem_len, d),
                                      (w_mem_k[l].astype(BF16), w_mem_v[l].astype(BF16)), (F32, F32))
                mk, mv = mk.reshape(bsz, mem_len, MEM_INNER), mv.reshape(bsz, mem_len, MEM_INNER)
                outs['mk'].append(mk.reshape(bsz, mem_len, MEM_HEADS, MEM_HEAD_DIM))
                outs['mv'].append(mv.reshape(bsz, mem_len, MEM_HEADS, MEM_HEAD_DIM))
                bb, tl = 1, min(1024, seq)
            (q,) = multi_linear(x2, (w_mem_q[l].astype(BF16),), (BF16,))
            o = cross_attend(q.reshape(bsz, seq, MEM_INNER), mk, mv, bb, tl).reshape(m, MEM_INNER)
            x2 = proj_ln([o], [w_mem_o[l].astype(BF16)], x2, ln_g[l, 1][None, :], ln_b[l, 1][None, :])
            x2 = ffn_ln(x2, ffn_w_gate[l].astype(BF16), ffn_w_up[l].astype(BF16), ffn_w_down[l].astype(BF16),
                        ln_g[l, 2][None, :], ln_b[l, 2][None, :])
        stk = lambda a: jnp.stack(a, axis=0)
        return x2.reshape(bsz, seq, d), {k_: (stk(v_) if v_ else None) for k_, v_ in outs.items()}

    y_prompt, p = run(x_prompt, False)
    y_sample, s = run(x_sample, True)
    return (y_prompt, y_sample, p['ssm'], p['conv'], p['sbk'], p['sbv'], p['hg'], p['lat'], p['rope'], p['mk'],
            p['mv'], s['ssm'], s['conv'], s['sbk'], s['sbv'], s['hg'], s['lat'], s['rope'])
```

```python
import functools

import numpy as np
import jax
import jax.numpy as jnp
from jax import lax
from jax.experimental import pallas as pl
from jax.experimental.pallas import tpu as pltpu

F32 = jnp.float32
BF16 = jnp.bfloat16

D_MODEL = 1024
DEPTH = 2
PAGE_SIZE = 128

SSM_HEAD_DIM = 64
SSM_HEADS = 16
SSM_INNER = SSM_HEADS * SSM_HEAD_DIM
SSM_GROUPS = 2
SSM_STATE = 128
SSM_CONV = 4
SSM_CONV_DIM = SSM_INNER + 2 * SSM_GROUPS * SSM_STATE
SSM_CHUNK = 128

SB_HEADS = 8
SB_KV_HEADS = 4
SB_HEAD_DIM = 64
SB_Q_DIM = SB_HEADS * SB_HEAD_DIM
SB_KV_DIM = SB_KV_HEADS * SB_HEAD_DIM

HG_HEADS = 4
HG_DK = 128
HG_DV = 128
HG_CHUNK = 32
HG_DIM = HG_HEADS * HG_DK

MLA_HEADS = 8
MLA_Q_LORA = 384
MLA_KV_LORA = 256
MLA_NOPE = 64
MLA_ROPE = 32
MLA_V = 64
MLA_SCALE = (MLA_NOPE + MLA_ROPE) ** -0.5
ROPE_THETA = 10000.0

MEM_HEADS = 4
MEM_HEAD_DIM = 128
MEM_INNER = MEM_HEADS * MEM_HEAD_DIM

DEEPNORM_ALPHA = (2 * DEPTH) ** 0.25

LANES = 128
VMEM_LIMIT_BYTES = 48 * 1024 * 1024
NEG_BIG = -1e30
SB_DEAD_LOG = -120.0
MLA_ROW_GROUPS = 2
DECODE_PAGES_PER_STEP = 32
DECODE_PAGE_GROUPS = 1


def _cparams(*sem):
    return pltpu.CompilerParams(dimension_semantics=sem, vmem_limit_bytes=VMEM_LIMIT_BYTES)


def _sigmoid(x):
    return 0.5 * jnp.tanh(0.5 * x) + 0.5


def _silu(x):
    return x * _sigmoid(x)


def _softplus(x):
    return jnp.maximum(x, 0.0) + jnp.log1p(jnp.exp(-jnp.abs(x)))


def _dot(a, b):
    return jnp.dot(a, b, preferred_element_type=F32)


def _dot_nt(a, b):
    return lax.dot_general(a, b, (((1,), (1,)), ((), ())), preferred_element_type=F32)


def _dot_tn(a, b):
    return lax.dot_general(a, b, (((0,), (0,)), ((), ())), preferred_element_type=F32)


def _split3(x):
    hi = x.astype(BF16)
    r1 = x - hi.astype(F32)
    mid = r1.astype(BF16)
    lo = (r1 - mid.astype(F32)).astype(BF16)
    return hi, mid, lo


def _ones_dot(mask01, x):
    hi, mid, lo = _split3(x)
    return _dot(mask01, hi) + _dot(mask01, mid) + _dot(mask01, lo)


def _dot_ones(x, mask01):
    hi, mid, lo = _split3(x)
    return _dot(hi, mask01) + _dot(mid, mask01) + _dot(lo, mask01)


def _layer_norm(x, g, b):
    xc = x - jnp.mean(x, axis=-1, keepdims=True)
    var = jnp.mean(xc * xc, axis=-1, keepdims=True)
    return xc * lax.rsqrt(var + 1e-5) * g + b


def _rms(x, w, eps=1e-6):
    return x * lax.rsqrt(jnp.mean(x * x, axis=-1, keepdims=True) + eps) * w


def _multi_linear_kernel(x_ref, *refs, n):
    x = x_ref[...].astype(BF16)
    for w_ref, o_ref in zip(refs[:n], refs[n:]):
        o_ref[...] = _dot(x, w_ref[...]).astype(o_ref.dtype)


def multi_linear(x, ws, out_dtypes, tm=512):
    m, k = x.shape
    tm = min(tm, m)
    n = len(ws)
    in_specs = [pl.BlockSpec((tm, k), lambda i: (i, 0))]
    in_specs += [pl.BlockSpec(w.shape, lambda i: (0, 0)) for w in ws]
    out_specs = [pl.BlockSpec((tm, w.shape[1]), lambda i: (i, 0)) for w in ws]
    out_shape = [jax.ShapeDtypeStruct((m, w.shape[1]), dt) for w, dt in zip(ws, out_dtypes)]
    return pl.pallas_call(
        functools.partial(_multi_linear_kernel, n=n),
        grid=(m // tm,), in_specs=in_specs, out_specs=out_specs, out_shape=out_shape,
        compiler_params=_cparams("parallel"), name="multi_linear")(x, *ws)


def _proj_ln_kernel(*refs, n):
    a_refs, w_refs = refs[:n], refs[n:2 * n]
    r_ref, g_ref, b_ref, o_ref = refs[2 * n:]
    y = _dot(a_refs[0][...].astype(BF16), w_refs[0][...])
    for a_ref, w_ref in zip(a_refs[1:], w_refs[1:]):
        y = y + _dot(a_ref[...].astype(BF16), w_ref[...])
    o_ref[...] = _layer_norm(DEEPNORM_ALPHA * r_ref[...] + y, g_ref[...], b_ref[...])


def proj_ln(a_list, w_list, resid, g, b, tm=512):
    m, d = resid.shape
    tm = min(tm, m)
    n = len(a_list)
    in_specs = [pl.BlockSpec((tm, a.shape[1]), lambda i: (i, 0)) for a in a_list]
    in_specs += [pl.BlockSpec(w.shape, lambda i: (0, 0)) for w in w_list]
    in_specs += [pl.BlockSpec((tm, d), lambda i: (i, 0)),
                 pl.BlockSpec((1, d), lambda i: (0, 0)), pl.BlockSpec((1, d), lambda i: (0, 0))]
    return pl.pallas_call(
        functools.partial(_proj_ln_kernel, n=n),
        grid=(m // tm,), in_specs=in_specs, out_specs=pl.BlockSpec((tm, d), lambda i: (i, 0)),
        out_shape=jax.ShapeDtypeStruct((m, d), F32),
        compiler_params=_cparams("parallel"), name="proj_ln")(*a_list, *w_list, resid, g, b)


def _ffn_kernel(x_ref, wg_ref, wu_ref, wd_ref, g_ref, b_ref, o_ref, acc_ref):
    j = pl.program_id(1)
    x = x_ref[...]
    xb = x.astype(BF16)
    gate = _dot(xb, wg_ref[...])
    up = _dot(xb, wu_ref[...])
    part = _dot((_silu(gate) * up).astype(BF16), wd_ref[...])

    @pl.when(j == 0)
    def _():
        acc_ref[...] = part

    @pl.when(j > 0)
    def _():
        acc_ref[...] += part

    @pl.when(j == pl.num_programs(1) - 1)
    def _():
        o_ref[...] = _layer_norm(DEEPNORM_ALPHA * x + acc_ref[...], g_ref[...], b_ref[...])


def ffn_ln(x, wg, wu, wd, g, b, tm=512, th=1408):
    m, d = x.shape
    hdim = wg.shape[1]
    tm = min(tm, m)
    return pl.pallas_call(
        _ffn_kernel,
        grid=(m // tm, hdim // th),
        in_specs=[pl.BlockSpec((tm, d), lambda i, j: (i, 0)),
                  pl.BlockSpec((d, th), lambda i, j: (0, j)),
                  pl.BlockSpec((d, th), lambda i, j: (0, j)),
                  pl.BlockSpec((th, d), lambda i, j: (j, 0)),
                  pl.BlockSpec((1, d), lambda i, j: (0, 0)),
                  pl.BlockSpec((1, d), lambda i, j: (0, 0))],
        out_specs=pl.BlockSpec((tm, d), lambda i, j: (i, 0)),
        out_shape=jax.ShapeDtypeStruct((m, d), F32),
        scratch_shapes=[pltpu.VMEM((tm, d), F32)],
        compiler_params=_cparams("parallel", "arbitrary"), name="ffn_ln")(x, wg, wu, wd, g, b)


def _cross_kernel(q_ref, k_ref, v_ref, o_ref, *, cached):
    scale = MEM_HEAD_DIM ** -0.5
    for h in range(MEM_HEADS):
        sl = slice(h * MEM_HEAD_DIM, (h + 1) * MEM_HEAD_DIM)
        qh = q_ref[:, :, sl]
        if cached:
            kh = k_ref[0, :, :, h, :].astype(BF16)
            vh = v_ref[0, :, :, h, :].astype(BF16)
        else:
            kh = k_ref[:, :, sl].astype(BF16)
            vh = v_ref[:, :, sl].astype(BF16)
        s = jnp.einsum('bqd,bkd->bqk', qh, kh, preferred_element_type=F32) * scale
        e = jnp.exp(s - jnp.max(s, axis=-1, keepdims=True))
        p = e / jnp.sum(e, axis=-1, keepdims=True)
        oh = jnp.einsum('bqk,bkd->bqd', p.astype(BF16), vh, preferred_element_type=F32)
        o_ref[:, :, sl] = oh.astype(o_ref.dtype)


def cross_attend(q, mk, mv, bb, tl, layer=None):
    bsz, seq, d = q.shape
    cached = layer is not None
    if cached:
        mlen = mk.shape[2]
        kv_spec = pl.BlockSpec((1, bb, mlen, MEM_HEADS, MEM_HEAD_DIM), lambda i, j: (layer, i, 0, 0, 0))
    else:
        mlen = mk.shape[1]
        kv_spec = pl.BlockSpec((bb, mlen, d), lambda i, j: (i, 0, 0))
    return pl.pallas_call(
        functools.partial(_cross_kernel, cached=cached),
        grid=(bsz // bb, seq // tl),
        in_specs=[pl.BlockSpec((bb, tl, d), lambda i, j: (i, j, 0)), kv_spec, kv_spec],
        out_specs=pl.BlockSpec((bb, tl, d), lambda i, j: (i, j, 0)),
        out_shape=jax.ShapeDtypeStruct((bsz, seq, d), BF16),
        compiler_params=_cparams("parallel", "arbitrary"), name="cross_attend")(q, mk, mv)


def _ssd_kernel(xbc_ref, z_ref, dt_ref, conv0_ref, s0_ref, cw_ref, cb_ref, dtb_ref, alog_ref, dsk_ref, nw_ref,
                y_ref, sfin_ref, state_scr, xpad_scr, y_scr, *, q, lv):
    c = pl.program_id(1)
    pad = 8

    @pl.when(c == 0)
    def _():
        state_scr[...] = s0_ref[0]
        xpad_scr[0:pad, :] = jnp.zeros((pad, SSM_CONV_DIM), F32)
        xpad_scr[pad - (SSM_CONV - 1):pad, :] = conv0_ref[0]

    xpad_scr[pad:pad + q, :] = xbc_ref[0]
    conv = cb_ref[...]
    for tap in range(SSM_CONV):
        back = SSM_CONV - 1 - tap
        conv = conv + cw_ref[tap:tap + 1, :] * xpad_scr[pad - back:pad - back + q, :]
    xpad_scr[pad - (SSM_CONV - 1):pad, :] = xpad_scr[pad + q - (SSM_CONV - 1):pad + q, :]

    u = _silu(conv)
    xs = u[:, :SSM_INNER]
    bm = u[:, SSM_INNER:SSM_INNER + SSM_GROUPS * SSM_STATE].astype(BF16)
    cm = u[:, SSM_INNER + SSM_GROUPS * SSM_STATE:].astype(BF16)

    row = lax.broadcasted_iota(jnp.int32, (q, q), 0)
    col = lax.broadcasted_iota(jnp.int32, (q, q), 1)
    tri = row >= col
    dt = _softplus(dt_ref[0] + dtb_ref[...])
    if lv < q:
        dt = jnp.where(lax.broadcasted_iota(jnp.int32, dt.shape, 0) < lv, dt, 0.0)
    d_a = dt * (-jnp.exp(alog_ref[...]))
    acum = _ones_dot(tri.astype(BF16), d_a)
    acum_t = acum.T
    dt_t = dt.T
    a_last = acum_t[:, q - 1:q]
    dend_t = jnp.exp(a_last - acum_t) * dt_t
    chunk_decay = jnp.exp(a_last)
    e_acum = jnp.exp(acum)
    xs_t = xs.T

    heads_per_group = SSM_HEADS // SSM_GROUPS
    for g in range(SSM_GROUPS):
        bm_g = bm[:, g * SSM_STATE:(g + 1) * SSM_STATE]
        cm_g = cm[:, g * SSM_STATE:(g + 1) * SSM_STATE]
        cb = _dot_nt(cm_g, bm_g)
        for r in range(heads_per_group):
            h = g * heads_per_group + r
            hs = slice(h * SSM_HEAD_DIM, (h + 1) * SSM_HEAD_DIM)
            diff = acum[:, h:h + 1] - acum_t[h:h + 1, :]
            wts = cb * jnp.exp(jnp.where(tri, diff, NEG_BIG)) * dt_t[h:h + 1, :]
            s_prev = state_scr[h]
            y_h = _dot(wts.astype(BF16), xs[:, hs].astype(BF16))
            y_h = y_h + _dot_nt(cm_g, s_prev.astype(BF16)) * e_acum[:, h:h + 1]
            y_scr[:, hs] = y_h
            xw = (xs_t[hs, :] * dend_t[h:h + 1, :]).astype(BF16)
            state_scr[h] = s_prev * chunk_decay[h:h + 1, :] + _dot(xw, bm_g)

    y = y_scr[...] + dsk_ref[...] * xs
    y_ref[0] = _rms(y * _silu(z_ref[0]), nw_ref[...]).astype(y_ref.dtype)

    @pl.when(c == pl.num_programs(1) - 1)
    def _():
        sfin_ref[0] = state_scr[...]


def ssd_mixer(xbc, z, dt, conv0, s0, conv_w, conv_b, dt_bias, a_log, d_rep, norm_w, lv, out_dtype):
    bsz, seq, _ = xbc.shape
    q = SSM_CHUNK
    nc = seq // q
    row3 = lambda b, c: (b, c, 0)
    full2 = lambda b, c: (0, 0)
    return pl.pallas_call(
        functools.partial(_ssd_kernel, q=q, lv=lv),
        grid=(bsz, nc),
        in_specs=[pl.BlockSpec((1, q, SSM_CONV_DIM), row3),
                  pl.BlockSpec((1, q, SSM_INNER), row3),
                  pl.BlockSpec((1, q, LANES), row3),
                  pl.BlockSpec((1, SSM_CONV - 1, SSM_CONV_DIM), lambda b, c: (b, 0, 0)),
                  pl.BlockSpec((1, SSM_HEADS, SSM_HEAD_DIM, SSM_STATE), lambda b, c: (b, 0, 0, 0)),
                  pl.BlockSpec((SSM_CONV, SSM_CONV_DIM), full2),
                  pl.BlockSpec((1, SSM_CONV_DIM), full2),
                  pl.BlockSpec((1, LANES), full2),
                  pl.BlockSpec((1, LANES), full2),
                  pl.BlockSpec((1, SSM_INNER), full2),
                  pl.BlockSpec((1, SSM_INNER), full2)],
        out_specs=[pl.BlockSpec((1, q, SSM_INNER), row3),
                   pl.BlockSpec((1, SSM_HEADS, SSM_HEAD_DIM, SSM_STATE), lambda b, c: (b, 0, 0, 0))],
        out_shape=[jax.ShapeDtypeStruct((bsz, seq, SSM_INNER), out_dtype),
                   jax.ShapeDtypeStruct((bsz, SSM_HEADS, SSM_HEAD_DIM, SSM_STATE), F32)],
        scratch_shapes=[pltpu.VMEM((SSM_HEADS, SSM_HEAD_DIM, SSM_STATE), F32),
                        pltpu.VMEM((q + 8, SSM_CONV_DIM), F32),
                        pltpu.VMEM((q, SSM_INNER), F32)],
        compiler_params=_cparams("parallel", "arbitrary"), name="ssd_mixer",
    )(xbc, z, dt, conv0, s0, conv_w, conv_b, dt_bias, a_log, d_rep, norm_w)


def _later_mask(n):
    r = lax.broadcasted_iota(jnp.int32, (n, n), 0)
    c = lax.broadcasted_iota(jnp.int32, (n, n), 1)
    return (r > c).astype(BF16)


def _sb_logs(s, valid):
    t = jnp.log(1.0 + jnp.exp(-jnp.abs(s)))
    log_beta = jnp.minimum(s, 0.0) - t
    log_keep = log_beta - s
    if valid is not None:
        log_keep = jnp.where(valid, log_keep, 0.0)
    return log_beta, log_keep


def _sb_kernel(q_ref, k_ref, v_ref, o_ref, k2_scr, v2_scr, q2_scr, carry_scr, acc_scr, *, tq):
    i = pl.program_id(1)
    seq = k_ref.shape[1]
    half = SB_HEAD_DIM
    hrows = 2 * tq
    rows = SB_KV_HEADS * hrows

    @pl.when(i == 0)
    def _():
        lane = lax.broadcasted_iota(jnp.int32, (seq, LANES), 1)
        for p in range(SB_KV_HEADS // 2):
            for src, dst in ((k_ref, k2_scr), (v_ref, v2_scr)):
                pair = src[0, :, p * LANES:(p + 1) * LANES]
                swapped = pltpu.roll(pair, half, 1)
                dst[2 * p] = jnp.where(lane < half, pair, swapped).astype(BF16)
                dst[2 * p + 1] = jnp.where(lane < half, swapped, pair).astype(BF16)

    lane_q = lax.broadcasted_iota(jnp.int32, (tq, LANES), 1)
    zero = jnp.zeros((), BF16)
    for kh in range(SB_KV_HEADS):
        qg = q_ref[0, :, kh * LANES:(kh + 1) * LANES] * jnp.asarray(SB_HEAD_DIM ** -0.5, BF16)
        q2_scr[kh * hrows:kh * hrows + tq, :] = jnp.where(lane_q < half, qg, zero)
        q2_scr[kh * hrows + tq:(kh + 1) * hrows, :] = jnp.where(lane_q >= half, qg, zero)

    tmat = _later_mask(tq)

    def tile(j, carry, acc, valid):
        start = pl.multiple_of(j * tq, tq)
        s = jnp.concatenate([_dot_nt(q2_scr[kh * hrows:(kh + 1) * hrows, :], k2_scr[kh, pl.ds(start, tq), :])
                             for kh in range(SB_KV_HEADS)], axis=0)
        log_beta, log_keep = _sb_logs(s, valid)
        w = jnp.exp(log_beta + _dot_ones(log_keep, tmat) + carry)
        if valid is not None:
            w = jnp.where(valid, w, 0.0)
        w = w.astype(BF16)
        pv = jnp.concatenate([_dot(w[kh * hrows:(kh + 1) * hrows], v2_scr[kh, pl.ds(start, tq), :])
                              for kh in range(SB_KV_HEADS)], axis=0)
        return carry + jnp.sum(log_keep, axis=-1, keepdims=True), acc + pv

    row = lax.broadcasted_iota(jnp.int32, (rows, tq), 0)
    col = lax.broadcasted_iota(jnp.int32, (rows, tq), 1)
    carry, acc = tile(i, jnp.zeros((rows, LANES), F32), jnp.zeros((rows, LANES), F32), col < row % tq)
    carry_scr[...] = carry
    acc_scr[...] = acc

    def cond(st):
        j, live = st
        return jnp.logical_and(j >= 0, live > SB_DEAD_LOG)

    def body(st):
        j, _ = st
        carry, acc = tile(j, carry_scr[...], acc_scr[...], None)
        carry_scr[...] = carry
        acc_scr[...] = acc
        return j - 1, jnp.max(carry)

    lax.while_loop(cond, body, (i - 1, jnp.max(carry)))
    for kh in range(SB_KV_HEADS):
        o_ref[0, :, kh * LANES:(kh + 1) * LANES] = jnp.where(
            lane_q < half, acc_scr[kh * hrows:kh * hrows + tq, :],
            acc_scr[kh * hrows + tq:(kh + 1) * hrows, :]).astype(o_ref.dtype)


def sb_attention(q, k, v, tq=128):
    bsz, seq, _ = q.shape
    rows = SB_KV_HEADS * 2 * tq
    return pl.pallas_call(
        functools.partial(_sb_kernel, tq=tq),
        grid=(bsz, seq // tq),
        in_specs=[pl.BlockSpec((1, tq, SB_Q_DIM), lambda b, i: (b, i, 0)),
                  pl.BlockSpec((1, seq, SB_KV_DIM), lambda b, i: (b, 0, 0)),
                  pl.BlockSpec((1, seq, SB_KV_DIM), lambda b, i: (b, 0, 0))],
        out_specs=pl.BlockSpec((1, tq, SB_Q_DIM), lambda b, i: (b, i, 0)),
        out_shape=jax.ShapeDtypeStruct((bsz, seq, SB_Q_DIM), BF16),
        scratch_shapes=[pltpu.VMEM((SB_KV_HEADS, seq, LANES), BF16),
                        pltpu.VMEM((SB_KV_HEADS, seq, LANES), BF16),
                        pltpu.VMEM((rows, LANES), BF16),
                        pltpu.VMEM((rows, LANES), F32),
                        pltpu.VMEM((rows, LANES), F32)],
        compiler_params=_cparams("parallel", "arbitrary"), name="sb_attention")(q, k, v)


def _sb_pages(q2, kts, vts, tmat, carry, acc, valid):
    rows = q2.shape[0]
    s = jnp.concatenate([_dot(q2, kt) for kt in kts], axis=0)
    log_beta, log_keep = _sb_logs(s, valid)
    later = _dot_ones(log_keep, tmat)
    total = jnp.sum(log_keep, axis=-1, keepdims=True)
    carries = []
    for u in range(len(kts)):
        carries.append(carry)
        carry = carry + total[u * rows:(u + 1) * rows]
    w = jnp.exp(log_beta + later + jnp.concatenate(carries, axis=0))
    if valid is not None:
        w = jnp.where(valid, w, 0.0)
    w = w.astype(BF16)
    for u, vt in enumerate(vts):
        acc = acc + _dot_nt(w[u * rows:(u + 1) * rows], vt)
    return carry, acc


def _sb_dec_kernel(pt_ref, q_ref, kn_ref, vn_ref, *rest, pp, lq):
    del pt_ref
    k_refs, v_refs = rest[:pp], rest[pp:2 * pp]
    o_ref, carry_scr, acc_scr = rest[2 * pp:]
    s = pl.program_id(1)
    q2 = q_ref[0]
    rows = q2.shape[0]
    tmat = _later_mask(PAGE_SIZE)

    @pl.when(s == 0)
    def _():
        r = lax.broadcasted_iota(jnp.int32, (rows, PAGE_SIZE), 0)
        cidx = lax.broadcasted_iota(jnp.int32, (rows, PAGE_SIZE), 1)
        valid = cidx < r % lq
        carry, acc = _sb_pages(q2, [kn_ref[0].astype(BF16)], [vn_ref[0].astype(BF16)], tmat,
                               jnp.zeros((rows, LANES), F32), jnp.zeros((rows, SB_KV_DIM), F32), valid)
        carry_scr[...] = carry
        acc_scr[...] = acc

    carry, acc = carry_scr[...], acc_scr[...]
    per = pp // min(DECODE_PAGE_GROUPS, pp)
    for g0 in range(0, pp, per):
        carry, acc = _sb_pages(q2, [r[0].astype(BF16) for r in k_refs[g0:g0 + per]],
                               [r[0].astype(BF16) for r in v_refs[g0:g0 + per]], tmat, carry, acc, None)
    carry_scr[...] = carry
    acc_scr[...] = acc

    @pl.when(s == pl.num_programs(1) - 1)
    def _():
        o_ref[0] = acc


def _pages_per_step(n_pages):
    pp = DECODE_PAGES_PER_STEP
    while n_pages % pp:
        pp //= 2
    return pp


def sb_decode(q_bd, kt_new, vt_new, kt_pool, vt_pool, page_table, lq):
    bsz, rows, _ = q_bd.shape
    n_pages = page_table.shape[1]
    pp = _pages_per_step(n_pages)

    def page_map(u):
        return lambda b, s, pt: (pt[b, n_pages - 1 - (s * pp + u)], 0, 0)

    per_b = lambda b, s, pt: (b, 0, 0)
    page_spec = [pl.BlockSpec((1, SB_KV_DIM, PAGE_SIZE), page_map(u)) for u in range(pp)]
    grid_spec = pltpu.PrefetchScalarGridSpec(
        num_scalar_prefetch=1, grid=(bsz, n_pages // pp),
        in_specs=[pl.BlockSpec((1, rows, SB_KV_DIM), per_b),
                  pl.BlockSpec((1, SB_KV_DIM, PAGE_SIZE), per_b),
                  pl.BlockSpec((1, SB_KV_DIM, PAGE_SIZE), per_b)] + page_spec + page_spec,
        out_specs=pl.BlockSpec((1, rows, SB_KV_DIM), per_b),
        scratch_shapes=[pltpu.VMEM((rows, LANES), F32), pltpu.VMEM((rows, SB_KV_DIM), F32)])
    return pl.pallas_call(
        functools.partial(_sb_dec_kernel, pp=pp, lq=lq), grid_spec=grid_spec,
        out_shape=jax.ShapeDtypeStruct((bsz, rows, SB_KV_DIM), F32),
        compiler_params=_cparams("parallel", "arbitrary"), name="sb_decode",
    )(page_table, q_bd, kt_new, vt_new, *([kt_pool] * pp), *([vt_pool] * pp))


def _hgrn_kernel(hq_ref, hf_ref, hi_ref, hg_ref, s0_ref, lb_ref, nw_ref, o_ref, sfin_ref, st_scr, o_scr, *, tl, lv):
    c = pl.program_id(1)
    nchunk = tl // HG_CHUNK

    @pl.when(c == 0)
    def _():
        for h in range(HG_HEADS):
            st_scr[h] = s0_ref[0, h].T

    lb = lb_ref[...]
    f_pre = hf_ref[0]
    sig = _sigmoid(f_pre)
    logf = jnp.log(lb + (1.0 - lb) * sig)
    kk = (1.0 - lb) * (1.0 - sig)
    if lv < tl:
        live = lax.broadcasted_iota(jnp.int32, logf.shape, 0) < lv
        logf = jnp.where(live, logf, 0.0)
        kk = jnp.where(live, kk, 0.0)
    row = lax.broadcasted_iota(jnp.int32, (tl, tl), 0)
    col = lax.broadcasted_iota(jnp.int32, (tl, tl), 1)
    same = (row // HG_CHUNK) == (col // HG_CHUNK)
    causal = jnp.logical_and(same, row >= col)
    bcum = _ones_dot(causal.astype(BF16), logf)
    btot = _ones_dot(same.astype(BF16), logf)
    q_dec = (hq_ref[0] * jnp.exp(bcum)).astype(BF16)
    k_inv = (kk * jnp.exp(-bcum)).astype(BF16)
    k_end = kk * jnp.exp(btot - bcum)
    e_tot = jnp.exp(btot)
    v = hi_ref[0].astype(BF16)
    chunk_of_row = lax.broadcasted_iota(jnp.int32, (tl, HG_DK), 0) // HG_CHUNK

    for h in range(HG_HEADS):
        hs = slice(h * HG_DK, (h + 1) * HG_DK)
        att = jnp.where(causal, _dot_nt(q_dec[:, hs], k_inv[:, hs]), 0.0)
        o_scr[:, hs] = _dot(att.astype(BF16), v[:, hs])
        k_blocks = jnp.concatenate([jnp.where(chunk_of_row == cc, k_end[:, hs], 0.0) for cc in range(nchunk)], axis=1)
        incr = _dot_tn(v[:, hs], k_blocks.astype(BF16))
        st = st_scr[h]
        for cc in range(nchunk):
            rs = slice(cc * HG_CHUNK, (cc + 1) * HG_CHUNK)
            o_scr[rs, hs] += _dot_nt(q_dec[rs, hs], st.astype(BF16))
            decay = e_tot[cc * HG_CHUNK:cc * HG_CHUNK + 1, hs]
            st = st * decay + incr[:, cc * HG_DK:(cc + 1) * HG_DK]
        st_scr[h] = st

    o = o_scr[...]
    gate = _silu(hg_ref[0])
    for h in range(HG_HEADS):
        hs = slice(h * HG_DV, (h + 1) * HG_DV)
        o_ref[0, :, hs] = (_rms(o[:, hs], nw_ref[:, hs]) * gate[:, hs]).astype(o_ref.dtype)

    @pl.when(c == pl.num_programs(1) - 1)
    def _():
        for h in range(HG_HEADS):
            sfin_ref[0, h] = st_scr[h].T


def hgrn_mixer(hq, hf, hi, hg, s0, lb, norm_w, tl, lv, out_dtype):
    bsz, seq, _ = hq.shape
    row3 = lambda b, c: (b, c, 0)
    full2 = lambda b, c: (0, 0)
    st4 = lambda b, c: (b, 0, 0, 0)
    return pl.pallas_call(
        functools.partial(_hgrn_kernel, tl=tl, lv=lv),
        grid=(bsz, seq // tl),
        in_specs=[pl.BlockSpec((1, tl, HG_DIM), row3)] * 4 + [
            pl.BlockSpec((1, HG_HEADS, HG_DK, HG_DV), st4),
            pl.BlockSpec((1, HG_DIM), full2), pl.BlockSpec((1, HG_DIM), full2)],
        out_specs=[pl.BlockSpec((1, tl, HG_DIM), row3), pl.BlockSpec((1, HG_HEADS, HG_DK, HG_DV), st4)],
        out_shape=[jax.ShapeDtypeStruct((bsz, seq, HG_DIM), out_dtype),
                   jax.ShapeDtypeStruct((bsz, HG_HEADS, HG_DK, HG_DV), F32)],
        scratch_shapes=[pltpu.VMEM((HG_HEADS, HG_DV, HG_DK), F32), pltpu.VMEM((tl, HG_DIM), F32)],
        compiler_params=_cparams("parallel", "arbitrary"), name="hgrn_mixer",
    )(hq, hf, hi, hg, s0, lb, norm_w)


def _mla_pre_kernel(cq_ref, ckv_ref, kra_ref, krb_ref, cosq_ref, sinq_ref, cosk_ref, sink_ref, qnw_ref, kvnw_ref,
                    wn_ref, wr_ref, wrs_ref, wukt_ref, ql_ref, qr_ref, c_ref, r_ref):
    n = _rms(cq_ref[...], qnw_ref[...]).astype(BF16)
    q_nope = _dot(n, wn_ref[...]).astype(BF16)
    q_rope = _dot(n, wr_ref[...]) * cosq_ref[...] + _dot(n, wrs_ref[...]) * sinq_ref[...]
    qr_ref[...] = q_rope.astype(qr_ref.dtype)
    for h in range(MLA_HEADS):
        ql_ref[:, h * MLA_KV_LORA:(h + 1) * MLA_KV_LORA] = _dot(
            q_nope[:, h * MLA_NOPE:(h + 1) * MLA_NOPE], wukt_ref[h]).astype(ql_ref.dtype)
    c_ref[...] = _rms(ckv_ref[...], kvnw_ref[...])
    r = kra_ref[...] * cosk_ref[...] + krb_ref[...] * sink_ref[...]
    r_ref[...] = r[:, :MLA_ROPE]


def mla_pre(cq, ckv, kra, krb, tabs, qnw, kvnw, wn, wr, wrs, wukt, tm=512):
    m = cq.shape[0]
    tm = min(tm, m, tabs[0].shape[0])
    nblk = tabs[0].shape[0] // tm
    rowb = lambda i: (i, 0)
    tabb = lambda i: (i % nblk, 0)
    full2 = lambda i: (0, 0)
    hq = MLA_HEADS * MLA_ROPE
    return pl.pallas_call(
        _mla_pre_kernel,
        grid=(m // tm,),
        in_specs=[pl.BlockSpec((tm, MLA_Q_LORA), rowb), pl.BlockSpec((tm, MLA_KV_LORA), rowb),
                  pl.BlockSpec((tm, LANES), rowb), pl.BlockSpec((tm, LANES), rowb),
                  pl.BlockSpec((tm, hq), tabb), pl.BlockSpec((tm, hq), tabb),
                  pl.BlockSpec((tm, LANES), tabb), pl.BlockSpec((tm, LANES), tabb),
                  pl.BlockSpec((1, MLA_Q_LORA), full2), pl.BlockSpec((1, MLA_KV_LORA), full2),
                  pl.BlockSpec(wn.shape, full2), pl.BlockSpec(wr.shape, full2), pl.BlockSpec(wrs.shape, full2),
                  pl.BlockSpec(wukt.shape, lambda i: (0, 0, 0))],
        out_specs=[pl.BlockSpec((tm, MLA_HEADS * MLA_KV_LORA), rowb), pl.BlockSpec((tm, hq), rowb),
                   pl.BlockSpec((tm, MLA_KV_LORA), rowb), pl.BlockSpec((tm, MLA_ROPE), rowb)],
        out_shape=[jax.ShapeDtypeStruct((m, MLA_HEADS * MLA_KV_LORA), BF16), jax.ShapeDtypeStruct((m, hq), BF16),
                   jax.ShapeDtypeStruct((m, MLA_KV_LORA), F32), jax.ShapeDtypeStruct((m, MLA_ROPE), F32)],
        compiler_params=_cparams("parallel"), name="mla_pre",
    )(cq, ckv, kra, krb, *tabs, qnw, kvnw, wn, wr, wrs, wukt)


def _mla_softmax_step(s_list, cblks, m, l, acc):
    def fold(x, op):
        parts = [x[:, b * LANES:(b + 1) * LANES] for b in range(x.shape[1] // LANES)]
        return functools.reduce(op, parts)

    smax = functools.reduce(jnp.maximum, [fold(s, jnp.maximum) for s in s_list])
    m_new = jnp.maximum(m, jnp.max(smax, axis=-1, keepdims=True))
    alpha = jnp.exp(m - m_new)
    l = alpha * l
    acc = jnp.concatenate([alpha] * (acc.shape[1] // LANES), axis=1) * acc
    for s, cblk in zip(s_list, cblks):
        p = jnp.exp(s - jnp.concatenate([m_new] * (s.shape[1] // LANES), axis=1))
        l = l + fold(p, jnp.add)
        acc = acc + _dot(p.astype(BF16), cblk)
    return m_new, l, acc


def _mla_kernel(ql_ref, qr_ref, c_ref, r_ref, wuv_ref, o_ref, qs_scr, qrs_scr, cb_scr, rb_scr, m_scr, l_scr, acc_scr,
                *, tq, tk):
    i = pl.program_id(1)

    @pl.when(i == 0)
    def _():
        cb_scr[...] = c_ref[0].astype(BF16)
        rb_scr[...] = r_ref[0].astype(BF16)

    for h in range(MLA_HEADS):
        qs_scr[h * tq:(h + 1) * tq, :] = ql_ref[0, :, h * MLA_KV_LORA:(h + 1) * MLA_KV_LORA]
        qrs_scr[h * tq:(h + 1) * tq, :] = qr_ref[0, :, h * MLA_ROPE:(h + 1) * MLA_ROPE]
    rows = MLA_HEADS * tq
    grows = rows // MLA_ROW_GROUPS

    def step(j, valid):
        start = pl.multiple_of(j * tk, tk)
        cblk = cb_scr[pl.ds(start, tk), :]
        rblk = rb_scr[pl.ds(start, tk), :]
        for g in range(MLA_ROW_GROUPS):
            rs = slice(g * grows, (g + 1) * grows)
            s = (_dot_nt(qs_scr[rs, :], cblk) + _dot_nt(qrs_scr[rs, :], rblk)) * MLA_SCALE
            if valid is None:
                m, l, acc = m_scr[rs, :], l_scr[rs, :], acc_scr[rs, :]
            else:
                s = jnp.where(valid, s, NEG_BIG)
                m, l = jnp.full((grows, LANES), NEG_BIG, F32), jnp.zeros((grows, LANES), F32)
                acc = jnp.zeros((grows, MLA_KV_LORA), F32)
            m_scr[rs, :], l_scr[rs, :], acc_scr[rs, :] = _mla_softmax_step([s], [cblk], m, l, acc)

    nfull = (i * tq) // tk
    q_idx = i * tq + lax.broadcasted_iota(jnp.int32, (grows, tk), 0) % tq
    k_idx = nfull * tk + lax.broadcasted_iota(jnp.int32, (grows, tk), 1)
    step(nfull, k_idx <= q_idx)

    def body(j, carry):
        step(j, None)
        return carry

    lax.fori_loop(0, nfull, body, 0)
    ctx = (acc_scr[...] / jnp.sum(l_scr[...], axis=-1, keepdims=True)).astype(BF16)
    for h in range(MLA_HEADS):
        o_ref[0, :, h * MLA_V:(h + 1) * MLA_V] = _dot(ctx[h * tq:(h + 1) * tq], wuv_ref[h]).astype(o_ref.dtype)


def mla_attention(ql, qr, c, r, wuv, tq=128, tk=512):
    bsz, seq, _ = ql.shape
    tk = min(tk, seq)
    rows = MLA_HEADS * tq
    return pl.pallas_call(
        functools.partial(_mla_kernel, tq=tq, tk=tk),
        grid=(bsz, seq // tq),
        in_specs=[pl.BlockSpec((1, tq, MLA_HEADS * MLA_KV_LORA), lambda b, i: (b, i, 0)),
                  pl.BlockSpec((1, tq, MLA_HEADS * MLA_ROPE), lambda b, i: (b, i, 0)),
                  pl.BlockSpec((1, seq, MLA_KV_LORA), lambda b, i: (b, 0, 0)),
                  pl.BlockSpec((1, seq, MLA_ROPE), lambda b, i: (b, 0, 0)),
                  pl.BlockSpec(wuv.shape, lambda b, i: (0, 0, 0))],
        out_specs=pl.BlockSpec((1, tq, MLA_HEADS * MLA_V), lambda b, i: (b, i, 0)),
        out_shape=jax.ShapeDtypeStruct((bsz, seq, MLA_HEADS * MLA_V), BF16),
        scratch_shapes=[pltpu.VMEM((rows, MLA_KV_LORA), BF16), pltpu.VMEM((rows, MLA_ROPE), BF16),
                        pltpu.VMEM((seq, MLA_KV_LORA), BF16), pltpu.VMEM((seq, MLA_ROPE), BF16),
                        pltpu.VMEM((rows, LANES), F32), pltpu.VMEM((rows, LANES), F32),
                        pltpu.VMEM((rows, MLA_KV_LORA), F32)],
        compiler_params=_cparams("parallel", "arbitrary"), name="mla_attention")(ql, qr, c, r, wuv)


def _mla_pages(qs, qrs, cs, rts, m, l, acc, valid):
    s_list = [(_dot_nt(qs, c) + _dot(qrs, rt)) * MLA_SCALE for c, rt in zip(cs, rts)]
    if valid is not None:
        s_list = [jnp.where(valid, s, NEG_BIG) for s in s_list]
    return _mla_softmax_step(s_list, cs, m, l, acc)


def _mla_dec_kernel(pt_ref, ql_ref, qr_ref, cn_ref, rn_ref, *rest, pp, lq):
    del pt_ref
    c_refs, r_refs = rest[:pp], rest[pp:2 * pp]
    o_ref, m_scr, l_scr, acc_scr = rest[2 * pp:]
    s = pl.program_id(1)
    qs = ql_ref[0]
    qrs = qr_ref[0]
    rows = qs.shape[0]

    @pl.when(s == 0)
    def _():
        r2 = lax.broadcasted_iota(jnp.int32, (rows, PAGE_SIZE), 0)
        c2 = lax.broadcasted_iota(jnp.int32, (rows, PAGE_SIZE), 1)
        m, l, acc = _mla_pages(qs, qrs, [cn_ref[0].astype(BF16)], [rn_ref[0].astype(BF16)],
                               jnp.full((rows, LANES), NEG_BIG, F32), jnp.zeros((rows, LANES), F32),
                               jnp.zeros((rows, MLA_KV_LORA), F32), c2 <= r2 % lq)
        m_scr[...] = m
        l_scr[...] = l
        acc_scr[...] = acc

    m, l, acc = m_scr[...], l_scr[...], acc_scr[...]
    per = pp // min(DECODE_PAGE_GROUPS, pp)
    for g0 in range(0, pp, per):
        m, l, acc = _mla_pages(qs, qrs, [r[0].astype(BF16) for r in c_refs[g0:g0 + per]],
                               [r[0].astype(BF16) for r in r_refs[g0:g0 + per]], m, l, acc, None)
    m_scr[...] = m
    l_scr[...] = l
    acc_scr[...] = acc

    @pl.when(s == pl.num_programs(1) - 1)
    def _():
        o_ref[0] = acc / jnp.sum(l, axis=-1, keepdims=True)


def mla_decode(ql, qr, c_new, rt_new, c_pool, rt_pool, page_table, lq):
    bsz, rows, _ = ql.shape
    n_pages = page_table.shape[1]
    pp = _pages_per_step(n_pages)

    def page_map(u):
        return lambda b, s, pt: (pt[b, s * pp + u], 0, 0)

    per_b = lambda b, s, pt: (b, 0, 0)
    grid_spec = pltpu.PrefetchScalarGridSpec(
        num_scalar_prefetch=1, grid=(bsz, n_pages // pp),
        in_specs=[pl.BlockSpec((1, rows, MLA_KV_LORA), per_b), pl.BlockSpec((1, rows, MLA_ROPE), per_b),
                  pl.BlockSpec((1, PAGE_SIZE, MLA_KV_LORA), per_b), pl.BlockSpec((1, MLA_ROPE, PAGE_SIZE), per_b)]
        + [pl.BlockSpec((1, PAGE_SIZE, MLA_KV_LORA), page_map(u)) for u in range(pp)]
        + [pl.BlockSpec((1, MLA_ROPE, PAGE_SIZE), page_map(u)) for u in range(pp)],
        out_specs=pl.BlockSpec((1, rows, MLA_KV_LORA), per_b),
        scratch_shapes=[pltpu.VMEM((rows, LANES), F32), pltpu.VMEM((rows, LANES), F32),
                        pltpu.VMEM((rows, MLA_KV_LORA), F32)])
    return pl.pallas_call(
        functools.partial(_mla_dec_kernel, pp=pp, lq=lq), grid_spec=grid_spec,
        out_shape=jax.ShapeDtypeStruct((bsz, rows, MLA_KV_LORA), F32),
        compiler_params=_cparams("parallel", "arbitrary"), name="mla_decode",
    )(page_table, ql, qr, c_new, rt_new, *([c_pool] * pp), *([rt_pool] * pp))


def _head_linear_kernel(x_ref, w_ref, o_ref):
    o_ref[0] = _dot(x_ref[0].astype(BF16), w_ref[0])


def head_linear(x, w):
    nh, m, k = x.shape
    n = w.shape[2]
    return pl.pallas_call(
        _head_linear_kernel, grid=(nh,),
        in_specs=[pl.BlockSpec((1, m, k), lambda h: (h, 0, 0)), pl.BlockSpec((1, k, n), lambda h: (h, 0, 0))],
        out_specs=pl.BlockSpec((1, m, n), lambda h: (h, 0, 0)),
        out_shape=jax.ShapeDtypeStruct((nh, m, n), F32),
        compiler_params=_cparams("parallel"), name="head_linear")(x, w)


def _pad_cols(w, n):
    return jnp.pad(w, ((0, 0), (0, n - w.shape[1])))


def _lane_row(v):
    return jnp.pad(v.astype(F32), (0, LANES - v.shape[0]))[None, :]


def _rope_tables(pos):
    half = MLA_ROPE // 2
    inv = ROPE_THETA ** (-jnp.arange(half, dtype=F32) / half)
    ang = pos[:, None] * inv[None, :]
    cos = jnp.concatenate([jnp.cos(ang), jnp.cos(ang)], axis=-1)
    sin = jnp.concatenate([-jnp.sin(ang), jnp.sin(ang)], axis=-1)
    cosq, sinq = jnp.tile(cos, (1, MLA_HEADS)), jnp.tile(sin, (1, MLA_HEADS))
    cosk, sink = _pad_cols(cos, LANES), _pad_cols(sin, LANES)
    return cosq, sinq, cosk, sink


def _swap_halves(w):
    half = MLA_ROPE // 2
    g = w.reshape(w.shape[0], -1, 2, half)
    return g[:, :, ::-1, :].reshape(w.shape)


def _even_weights(w_in, w_out):
    cuts = np.cumsum([SSM_INNER, SSM_CONV_DIM, SSM_HEADS, SB_Q_DIM, SB_KV_DIM, SB_KV_DIM])[:-1].tolist()
    wz, wxbc, wdt, wq, wk, wv = jnp.split(w_in.astype(BF16), cuts, axis=1)
    wo = w_out.astype(BF16)
    return (wz, wxbc, _pad_cols(wdt, LANES), wq, wk, wv), (wo[:SSM_INNER], wo[SSM_INNER:])


def _odd_weights(w_in, w_out):
    cuts = np.cumsum([HG_DIM, HG_DIM, HG_DIM, HG_DIM, MLA_Q_LORA, MLA_KV_LORA, MLA_ROPE])[:-1].tolist()
    whq, whf, whi, whg, wcq, wckv, wkr = jnp.split(w_in.astype(BF16), cuts, axis=1)
    wo = w_out.astype(BF16)
    ws = (whq, whf, whi, whg, wcq, wckv, _pad_cols(wkr, LANES), _pad_cols(_swap_halves(wkr), LANES))
    return ws, (wo[:HG_DIM], wo[HG_DIM:])


def _pad_rows(a, n):
    return jnp.pad(a, ((0, 0), (0, n - a.shape[1]), (0, 0)))


def _transposed_pages(pool):
    npool = pool.shape[0]
    return jnp.transpose(pool, (0, 2, 3, 1)).reshape(npool, -1, PAGE_SIZE)


def _even_layer(x, sample, page_table, w_in, w_out, conv_w, conv_b, dt_bias, a_log, d_skip, norm_w,
                conv0, s0, k_pool, v_pool):
    bsz, seq, d = x.shape
    ws, (wo_a, wo_b) = _even_weights(w_in, w_out)
    z, xbc, dt, q, k, v = multi_linear(x.reshape(bsz * seq, d), ws, (F32, F32, F32, BF16, F32, F32))
    xbc3, z3, dt3 = (a.reshape(bsz, seq, -1) for a in (xbc, z, dt))
    conv_new = jnp.concatenate([conv0, xbc3], axis=1)[:, -(SSM_CONV - 1):]
    lv = seq
    if seq < SSM_CHUNK:
        lv = seq
        xbc3, z3, dt3 = (_pad_rows(a, SSM_CHUNK) for a in (xbc3, z3, dt3))
    y_a, s_new = ssd_mixer(xbc3, z3, dt3, conv0, s0, conv_w, conv_b[None, :], _lane_row(dt_bias), _lane_row(a_log),
                           jnp.repeat(d_skip.astype(F32), SSM_HEAD_DIM)[None, :], norm_w[None, :],
                           min(lv, SSM_CHUNK), F32 if sample else BF16)
    y_a = y_a[:, :seq].reshape(bsz * seq, SSM_INNER)
    q3 = q.reshape(bsz, seq, SB_Q_DIM)
    k3, v3 = k.reshape(bsz, seq, SB_KV_DIM), v.reshape(bsz, seq, SB_KV_DIM)
    if sample:
        group = SB_HEADS // SB_KV_HEADS
        q5 = (q3 * jnp.asarray(SB_HEAD_DIM ** -0.5, BF16)).reshape(bsz, seq, SB_KV_HEADS, group, SB_HEAD_DIM)
        q5 = jnp.transpose(q5, (0, 2, 3, 1, 4)).reshape(bsz, SB_KV_HEADS, group * seq, SB_HEAD_DIM)
        q_bd = jnp.einsum('bkxd,kj->bkxjd', q5, jnp.eye(SB_KV_HEADS, dtype=BF16))
        q_bd = q_bd.reshape(bsz, SB_KV_HEADS * group * seq, SB_KV_DIM)
        tr = lambda a: jnp.transpose(_pad_rows(a, PAGE_SIZE), (0, 2, 1))
        acc = sb_decode(q_bd, tr(k3), tr(v3), _transposed_pages(k_pool), _transposed_pages(v_pool), page_table, seq)
        acc = acc.reshape(bsz, SB_KV_HEADS, group, seq, SB_KV_HEADS, SB_HEAD_DIM)
        y_b = jnp.einsum('bkrtkd->btkrd', acc).reshape(bsz * seq, SB_Q_DIM)
    else:
        y_b = sb_attention(q3, k3, v3).reshape(bsz * seq, SB_Q_DIM)
    return (y_a, y_b), (wo_a, wo_b), s_new, conv_new, k3.reshape(bsz, seq, SB_KV_HEADS, SB_HEAD_DIM), \
        v3.reshape(bsz, seq, SB_KV_HEADS, SB_HEAD_DIM)


def _odd_layer(x, sample, pos_start, page_table, lb, w_in, w_out, hg_norm_w, q_norm_w, kv_norm_w, w_uq, w_uk, w_uv,
               s0, c_pool, r_pool):
    bsz, seq, d = x.shape
    m = bsz * seq
    ws, (wo_c, wo_d) = _odd_weights(w_in, w_out)
    hq, hf, hi, hg, cq, ckv, kra, krb = multi_linear(x.reshape(m, d), ws, (F32,) * 8)
    h3 = [a.reshape(bsz, seq, HG_DIM) for a in (hq, hf, hi, hg)]
    tl = min(256, seq)
    lv = tl
    if seq < HG_CHUNK:
        tl, lv = HG_CHUNK, seq
        h3 = [_pad_rows(a, HG_CHUNK) for a in h3]
    o_c, s_new = hgrn_mixer(*h3, s0, lb[None, :].astype(F32), hg_norm_w.reshape(1, HG_DIM), tl, lv,
                            F32 if sample else BF16)
    o_c = o_c[:, :seq].reshape(m, HG_DIM)
    pos = jnp.arange(seq, dtype=F32) + pos_start
    tabs = _rope_tables(pos)
    if sample:
        tabs = tuple(jnp.tile(t, (bsz, 1)) for t in tabs)
    w_uq_b = w_uq.astype(BF16)
    wn = w_uq_b[:, :, :MLA_NOPE].reshape(MLA_Q_LORA, MLA_HEADS * MLA_NOPE)
    wr = w_uq_b[:, :, MLA_NOPE:].reshape(MLA_Q_LORA, MLA_HEADS * MLA_ROPE)
    wukt = jnp.transpose(w_uk.astype(BF16), (1, 2, 0))
    wuv = jnp.transpose(w_uv.astype(BF16), (1, 0, 2))
    ql, qr, c_new, r_new = mla_pre(cq, ckv, kra, krb, tabs, q_norm_w[None, :], kv_norm_w[None, :],
                                   wn, wr, _swap_halves(wr), wukt)
    c3, r3 = c_new.reshape(bsz, seq, MLA_KV_LORA), r_new.reshape(bsz, seq, MLA_ROPE)
    if sample:
        ql3 = jnp.transpose(ql.reshape(bsz, seq, MLA_HEADS, MLA_KV_LORA), (0, 2, 1, 3))
        qr3 = jnp.transpose(qr.reshape(bsz, seq, MLA_HEADS, MLA_ROPE), (0, 2, 1, 3))
        ctx = mla_decode(ql3.reshape(bsz, MLA_HEADS * seq, MLA_KV_LORA), qr3.reshape(bsz, MLA_HEADS * seq, MLA_ROPE),
                         _pad_rows(c3, PAGE_SIZE), jnp.transpose(_pad_rows(r3, PAGE_SIZE), (0, 2, 1)), c_pool,
                         jnp.transpose(r_pool, (0, 2, 1)), page_table, seq)
        ctx = jnp.transpose(ctx.reshape(bsz, MLA_HEADS, seq, MLA_KV_LORA), (1, 0, 2, 3))
        o_d = head_linear(ctx.reshape(MLA_HEADS, m, MLA_KV_LORA), wuv)
        o_d = jnp.transpose(o_d, (1, 0, 2)).reshape(m, MLA_HEADS * MLA_V)
    else:
        o_d = mla_attention(ql.reshape(bsz, seq, -1), qr.reshape(bsz, seq, -1), c3, r3, wuv)
        o_d = o_d.reshape(m, MLA_HEADS * MLA_V)
    return (o_c, o_d), (wo_c, wo_d), s_new, c3, r3


def _hgrn_lower_bound(gamma, layer):
    p = jax.nn.softmax(gamma.astype(F32), axis=0)
    return jnp.cumsum(p, axis=0)[layer] - p[0]


def kernel(x_prompt, x_sample, mem_prompt, state_ssm, state_conv, cache_sb_k, cache_sb_v, state_hgrn,
           cache_mla_latent, cache_mla_rope, cache_mem_k, cache_mem_v, page_table,
           w_in_even, w_out_even, ssm_conv_w, ssm_conv_b, ssm_dt_bias, ssm_a_log, ssm_d, ssm_norm_w,
           w_in_odd, w_out_odd, hg_lower_bound, hg_norm_w, mla_q_norm_w, mla_kv_norm_w, mla_w_uq, mla_w_uk, mla_w_uv,
           w_mem_q, w_mem_k, w_mem_v, w_mem_o, ffn_w_gate, ffn_w_up, ffn_w_down, ln_g, ln_b):
    past_len = page_table.shape[1] * PAGE_SIZE
    mem_len = mem_prompt.shape[1]

    def run(x, sample):
        bsz, seq, d = x.shape
        m = bsz * seq
        pos_start = past_len if sample else 0
        outs = dict(ssm=[], conv=[], sbk=[], sbv=[], hg=[], lat=[], rope=[], mk=[], mv=[])
        x2 = x.reshape(m, d)
        for l in range(DEPTH):
            i = l // 2
            if l % 2 == 0:
                if sample:
                    conv0, s0 = state_conv[i], state_ssm[i]
                else:
                    conv0 = jnp.zeros((bsz, SSM_CONV - 1, SSM_CONV_DIM), F32)
                    s0 = jnp.zeros((bsz, SSM_HEADS, SSM_HEAD_DIM, SSM_STATE), F32)
                ys, wos, s1, conv1, k_new, v_new = _even_layer(
                    x2.reshape(bsz, seq, d), sample, page_table, w_in_even[i], w_out_even[i], ssm_conv_w[i],
                    ssm_conv_b[i], ssm_dt_bias[i], ssm_a_log[i], ssm_d[i], ssm_norm_w[i], conv0, s0,
                    cache_sb_k[i], cache_sb_v[i])
                outs['ssm'].append(s1)
                outs['conv'].append(conv1)
                outs['sbk'].append(k_new)
                outs['sbv'].append(v_new)
            else:
                s0 = state_hgrn[i] if sample else jnp.zeros((bsz, HG_HEADS, HG_DK, HG_DV), F32)
                ys, wos, s1, c_new, r_new = _odd_layer(
                    x2.reshape(bsz, seq, d), sample, pos_start, page_table, _hgrn_lower_bound(hg_lower_bound, l),
                    w_in_odd[i], w_out_odd[i], hg_norm_w[i], mla_q_norm_w[i], mla_kv_norm_w[i], mla_w_uq[i],
                    mla_w_uk[i], mla_w_uv[i], s0, cache_mla_latent[i], cache_mla_rope[i])
                outs['hg'].append(s1)
                outs['lat'].append(c_new)
                outs['rope'].append(r_new)
            x2 = proj_ln(list(ys), list(wos), x2, ln_g[l, 0][None, :], ln_b[l, 0][None, :])
            if sample:
                mk, mv, layer = cache_mem_k, cache_mem_v, l
                bb, tl = 4, seq
            else:
                layer = None
                mk, mv = multi_linear(mem_prompt.reshape(bsz * mem_len, d),
                                      (w_mem_k[l].astype(BF16), w_mem_v[l].astype(BF16)), (F32, F32))
                mk, mv = mk.reshape(bsz, mem_len, MEM_INNER), mv.reshape(bsz, mem_len, MEM_INNER)
                outs['mk'].append(mk.reshape(bsz, mem_len, MEM_HEADS, MEM_HEAD_DIM))
                outs['mv'].append(mv.reshape(bsz, mem_len, MEM_HEADS, MEM_HEAD_DIM))
                bb, tl = 1, min(1024, seq)
            (q,) = multi_linear(x2, (w_mem_q[l].astype(BF16),), (BF16,))
            o = cross_attend(q.reshape(bsz, seq, MEM_INNER), mk, mv, bb, tl, layer).reshape(m, MEM_INNER)
            x2 = proj_ln([o], [w_mem_o[l].astype(BF16)], x2, ln_g[l, 1][None, :], ln_b[l, 1][None, :])
            x2 = ffn_ln(x2, ffn_w_gate[l].astype(BF16), ffn_w_up[l].astype(BF16), ffn_w_down[l].astype(BF16),
                        ln_g[l, 2][None, :], ln_b[l, 2][None, :])
        stk = lambda a: jnp.stack(a, axis=0)
        return x2.reshape(bsz, seq, d), {k_: (stk(v_) if v_ else None) for k_, v_ in outs.items()}

    y_prompt, p = run(x_prompt, False)
    y_sample, s = run(x_sample, True)
    return (y_prompt, y_sample, p['ssm'], p['conv'], p['sbk'], p['sbv'], p['hg'], p['lat'], p['rope'], p['mk'],
            p['mv'], s['ssm'], s['conv'], s['sbk'], s['sbv'], s['hg'], s['lat'], s['rope'])
```
